```python
import math
import jax, jax.numpy as jnp
from jax import lax
import numpy as np

D_MODEL = 1024
BATCH = 8
SEQ = 4096
DEPTH = 2

D_FF = 3 * D_MODEL
FFN_CONV = 3
N_BRANCH = 3
LRU_WIDTH = D_MODEL // 2
LRU_BLOCKS = 8
LRU_BLOCK_DIM = LRU_WIDTH // LRU_BLOCKS
LRU_CONV = 4
LRU_C = 8.0
DIFF_HEADS = 4
DIFF_HEAD_DIM = D_MODEL // 16
DIFF_WIDTH = DIFF_HEADS * 2 * DIFF_HEAD_DIM
ROT_DIM = DIFF_HEAD_DIM // 4
ROPE_THETA = 500000.0
Q_BLOCK = 128
RWKV_HEAD_DIM = 64
RWKV_WIDTH = D_MODEL // 2
RWKV_HEADS = RWKV_WIDTH // RWKV_HEAD_DIM
DECAY_LORA = 64
AAA_LORA = 64
GATE_LORA = 128
MV_LORA = 32
RWKV_COLS = 3 * RWKV_WIDTH + DECAY_LORA + AAA_LORA + GATE_LORA
NORM_EPS = 1e-6
SUBLN_EPS = 1e-5
GN_EPS = 64e-5
O_LX = N_BRANCH * D_MODEL
O_LY = O_LX + LRU_WIDTH
O_Q = O_LY + LRU_WIDTH
O_K = O_Q + DIFF_WIDTH
O_V = O_K + DIFF_WIDTH
O_RW = O_V + DIFF_WIDTH
IN_COLS = O_RW + RWKV_COLS

kernel_name = "hybrid_rglru_diffattn_rwkv7_adaln_block"


def rmsnorm(x, g, eps=NORM_EPS):
    x32 = x.astype(jnp.float32)
    y = x32 * lax.rsqrt(jnp.mean(x32 * x32, axis=-1, keepdims=True) + eps)
    return (y * g.astype(jnp.float32)).astype(x.dtype)


def causal_dwconv(x, w, b):
    K, C = w.shape
    y = lax.conv_general_dilated(x, w[:, None, :].astype(x.dtype), window_strides=(1,),
                                 padding=[(K - 1, 0)], dimension_numbers=("NWC", "WIO", "NWC"),
                                 feature_group_count=C)
    return y + b


def token_shift_mix(z, mu):
    prev = jnp.pad(z, ((0, 0), (1, 0), (0, 0)))[:, :-1]
    return z + (prev - z) * mu


def _linear_comb(e1, e2):
    a1, b1 = e1
    a2, b2 = e2
    return a1 * a2, a2 * b1 + b2


def rglru_branch(zx, zy, conv_w, conv_b, wa, ba, wx, bx, lam):
    xc = causal_dwconv(zx, conv_w, conv_b)
    B, S, W = xc.shape
    xb = xc.reshape(B, S, LRU_BLOCKS, LRU_BLOCK_DIM)
    r = jax.nn.sigmoid(jnp.einsum("bsnc,ncd->bsnd", xb, wa) + ba).reshape(B, S, W)
    i = jax.nn.sigmoid(jnp.einsum("bsnc,ncd->bsnd", xb, wx) + bx).reshape(B, S, W)
    log_a = -LRU_C * r.astype(jnp.float32) * jax.nn.softplus(-lam.astype(jnp.float32))
    a = jnp.exp(log_a)
    u = jnp.sqrt(-jnp.expm1(2.0 * log_a)) * (i * xc).astype(jnp.float32)
    _, h = lax.associative_scan(_linear_comb, (a, u), axis=1)
    return h.astype(zx.dtype) * jax.nn.gelu(zy, approximate=True)


def partial_rope(t, cos, sin):
    half = ROT_DIM // 2
    x1, x2, rest = t[..., :half], t[..., half:ROT_DIM], t[..., ROT_DIM:]
    return jnp.concatenate([x1 * cos - x2 * sin, x2 * cos + x1 * sin, rest], axis=-1)


def diff_attention(zq, zk, zv, positions, lq1, lk1, lq2, lk2, subln_g, lambda_init):
    B, S, _ = zq.shape
    H, d = DIFF_HEADS, DIFF_HEAD_DIM
    inv_freq = ROPE_THETA ** (-jnp.arange(0, ROT_DIM, 2, dtype=jnp.float32) / ROT_DIM)
    ang = positions.astype(jnp.float32)[..., None] * inv_freq
    cos = jnp.cos(ang)[:, :, None, None, :]
    sin = jnp.sin(ang)[:, :, None, None, :]
    q = partial_rope(zq.reshape(B, S, H, 2, d).astype(jnp.float32), cos, sin) * (d ** -0.5)
    k = partial_rope(zk.reshape(B, S, H, 2, d).astype(jnp.float32), cos, sin)
    v = zv.reshape(B, S, H, 2 * d)
    lam = (jnp.exp(jnp.sum(lq1.astype(jnp.float32) * lk1.astype(jnp.float32)))
           - jnp.exp(jnp.sum(lq2.astype(jnp.float32) * lk2.astype(jnp.float32))) + lambda_init)
    nblk = S // Q_BLOCK
    qb = q.reshape(B, nblk, Q_BLOCK, H, 2, d).transpose(1, 0, 2, 3, 4, 5)
    k_idx = jnp.arange(S)

    def block(args):
        q_blk, bi = args
        s = jnp.einsum("bqhcd,bkhcd->bhcqk", q_blk, k)
        q_idx = bi * Q_BLOCK + jnp.arange(Q_BLOCK)
        s = jnp.where(k_idx[None, :] <= q_idx[:, None], s, -1e30)
        p = jax.nn.softmax(s, axis=-1)
        p = p[:, :, 0] - lam * p[:, :, 1]
        return jnp.einsum("bhqk,bkhe->bqhe", p.astype(v.dtype), v)

    o = lax.map(block, (qb, jnp.arange(nblk)))
    o = o.transpose(1, 0, 2, 3, 4).reshape(B, S, H, 2 * d)
    o = rmsnorm(o, subln_g, SUBLN_EPS) * (1.0 - lambda_init)
    return o.reshape(B, S, DIFF_WIDTH)


def wkv7_scan(r, w, k, v, a, b):
    B, S, H, N = r.shape
    seq = tuple(jnp.swapaxes(t.astype(jnp.float32), 0, 1) for t in (r, w, k, v, a, b))

    def step(state, inp):
        r_t, w_t, k_t, v_t, a_t, b_t = inp
        sa = jnp.einsum("bhvk,bhk->bhv", state, a_t)
        state = (state * w_t[:, :, None, :] + sa[..., None] * b_t[:, :, None, :]
                 + v_t[..., None] * k_t[:, :, None, :])
        return state, jnp.einsum("bhvk,bhk->bhv", state, r_t)

    _, out = lax.scan(step, jnp.zeros((B, H, N, N), jnp.float32), seq)
    return jnp.swapaxes(out, 0, 1)


def rwkv7_mix(z, mu, w0, w2, a0, a2, g2, k_k, k_a, r_k, lnx_g, lnx_b, v_first,
              zvd=None, v_mu=None, v0=None, v2=None):
    B, S, _ = z.shape
    H, N, W = RWKV_HEADS, RWKV_HEAD_DIM, RWKV_WIDTH
    z = token_shift_mix(z, mu)
    r, k, v = z[..., :W], z[..., W:2 * W], z[..., 2 * W:3 * W]
    o = 3 * W
    wd = z[..., o:o + DECAY_LORA]
    o += DECAY_LORA
    ad = z[..., o:o + AAA_LORA]
    o += AAA_LORA
    gd = z[..., o:o + GATE_LORA]
    log_w = -jax.nn.softplus(-(w0 + jnp.tanh(wd) @ w2).astype(jnp.float32)) - 0.5
    decay = jnp.exp(-jnp.exp(log_w))
    a = jax.nn.sigmoid(a0 + ad @ a2)
    g = jax.nn.sigmoid(gd) @ g2
    if v_first is None:
        v_first = v
    else:
        v = v + (v_first - v) * jax.nn.sigmoid(v0 + token_shift_mix(zvd, v_mu) @ v2)
    kk = (k * k_k).astype(jnp.float32).reshape(B, S, H, N)
    kk = kk / jnp.maximum(jnp.sqrt(jnp.sum(kk * kk, axis=-1, keepdims=True)), 1e-12)
    k = k * (1 + (a - 1) * k_a)
    rh, kh, vh, ah = (t.reshape(B, S, H, N) for t in (r, k, v, a))
    out = wkv7_scan(rh, decay.reshape(B, S, H, N), kh, vh, -kk, kk * ah.astype(jnp.float32))
    mean = jnp.mean(out, axis=-1, keepdims=True)
    var = jnp.mean(jnp.square(out - mean), axis=-1, keepdims=True)
    out = ((out - mean) * lax.rsqrt(var + GN_EPS)).reshape(B, S, W)
    out = out * lnx_g.astype(jnp.float32) + lnx_b.astype(jnp.float32)
    bonus = jnp.sum((rh * kh * r_k).astype(jnp.float32), axis=-1, keepdims=True) * vh.astype(jnp.float32)
    y = (out + bonus.reshape(B, S, W)) * g.astype(jnp.float32)
    return y.astype(z.dtype), v_first


def conv_gated_mlp(h, w_up, conv_w, conv_b, w_down):
    u = causal_dwconv(h @ w_up, conv_w, conv_b)
    gate, val = u[..., :D_FF], u[..., D_FF:]
    return (jax.nn.gelu(gate, approximate=True) * val) @ w_down


def setup_inputs(seed: int = 0) -> dict:
    key = jax.random.key(seed)
    ks = iter(jax.random.split(key, 64))
    L, D = DEPTH, D_MODEL

    def nrm(shape, scale):
        return scale * jax.random.normal(next(ks), shape, jnp.float32)

    def uni(shape, lo, hi):
        return jax.random.uniform(next(ks), shape, jnp.float32, lo, hi)

    offsets = jax.random.randint(next(ks), (BATCH, 1), 0, 1024, dtype=jnp.int32)
    lru_u = uni((L, LRU_WIDTH), 0.9, 0.999)
    return {
        "x": nrm((BATCH, SEQ, D), 1.0),
        "c": nrm((BATCH, D), 1.0),
        "positions": offsets + jnp.arange(SEQ, dtype=jnp.int32)[None, :],
        "w_mod": nrm((L, D, 6 * D), 0.5 * D ** -0.5),
        "b_mod": nrm((L, 6 * D), 0.01),
        "norm1_g": 1.0 + nrm((L, D), 0.02),
        "norm2_g": 1.0 + nrm((L, D), 0.02),
        "final_g": 1.0 + nrm((D,), 0.02),
        "w_in": nrm((L, D, IN_COLS), D ** -0.5),
        "w_merge_a": nrm((L, LRU_WIDTH, D), LRU_WIDTH ** -0.5),
        "w_merge_b": nrm((L, DIFF_WIDTH, D), DIFF_WIDTH ** -0.5),
        "w_merge_c": nrm((L, RWKV_WIDTH, D), RWKV_WIDTH ** -0.5),
        "w_out": nrm((L, D, D), D ** -0.5),
        "lru_conv_w": nrm((L, LRU_CONV, LRU_WIDTH), 0.5),
        "lru_conv_b": nrm((L, LRU_WIDTH), 0.01),
        "lru_wa": nrm((L, LRU_BLOCKS, LRU_BLOCK_DIM, LRU_BLOCK_DIM), LRU_BLOCK_DIM ** -0.5),
        "lru_ba": nrm((L, LRU_BLOCKS, LRU_BLOCK_DIM), 0.01),
        "lru_wx": nrm((L, LRU_BLOCKS, LRU_BLOCK_DIM, LRU_BLOCK_DIM), LRU_BLOCK_DIM ** -0.5),
        "lru_bx": nrm((L, LRU_BLOCKS, LRU_BLOCK_DIM), 0.01),
        "lru_lambda": jnp.log(lru_u) - jnp.log1p(-lru_u),
        "diff_lq1": nrm((L, DIFF_HEAD_DIM), 0.1),
        "diff_lk1": nrm((L, DIFF_HEAD_DIM), 0.1),
        "diff_lq2": nrm((L, DIFF_HEAD_DIM), 0.1),
        "diff_lk2": nrm((L, DIFF_HEAD_DIM), 0.1),
        "diff_subln_g": 1.0 + nrm((L, 2 * DIFF_HEAD_DIM), 0.02),
        "rwkv_mu": uni((L, RWKV_COLS), 0.0, 1.0),
        "rwkv_w0": uni((L, RWKV_WIDTH), -6.5, -1.0),
        "rwkv_w2": nrm((L, DECAY_LORA, RWKV_WIDTH), 0.1),
        "rwkv_a0": nrm((L, RWKV_WIDTH), 0.01),
        "rwkv_a2": nrm((L, AAA_LORA, RWKV_WIDTH), 0.5 * AAA_LORA ** -0.5),
        "rwkv_g2": nrm((L, GATE_LORA, RWKV_WIDTH), GATE_LORA ** -0.5),
        "rwkv_kk": 0.85 + nrm((L, RWKV_WIDTH), 0.02),
        "rwkv_ka": 1.0 + nrm((L, RWKV_WIDTH), 0.02),
        "rwkv_rk": nrm((L, RWKV_HEADS, RWKV_HEAD_DIM), 0.1),
        "rwkv_lnx_g": 1.0 + nrm((L, RWKV_WIDTH), 0.02),
        "rwkv_lnx_b": nrm((L, RWKV_WIDTH), 0.01),
        "rwkv_v0": 1.0 + nrm((L - 1, RWKV_WIDTH), 0.02),
        "rwkv_v1": nrm((L - 1, D, MV_LORA), D ** -0.5),
        "rwkv_v2": nrm((L - 1, MV_LORA, RWKV_WIDTH), 0.5 * MV_LORA ** -0.5),
        "rwkv_vmu": uni((L - 1, MV_LORA), 0.0, 1.0),
        "ffn_up": nrm((L, D, 2 * D_FF), D ** -0.5),
        "ffn_conv_w": nrm((L, FFN_CONV, 2 * D_FF), 0.5),
        "ffn_conv_b": nrm((L, 2 * D_FF), 0.01),
        "ffn_down": nrm((L, D_FF, D), D_FF ** -0.5),
    }


def reference(x, c, positions, w_mod, b_mod, norm1_g, norm2_g, final_g, w_in,
              w_merge_a, w_merge_b, w_merge_c, w_out,
              lru_conv_w, lru_conv_b, lru_wa, lru_ba, lru_wx, lru_bx, lru_lambda,
              diff_lq1, diff_lk1, diff_lq2, diff_lk2, diff_subln_g,
              rwkv_mu, rwkv_w0, rwkv_w2, rwkv_a0, rwkv_a2, rwkv_g2, rwkv_kk, rwkv_ka,
              rwkv_rk, rwkv_lnx_g, rwkv_lnx_b, rwkv_v0, rwkv_v1, rwkv_v2, rwkv_vmu,
              ffn_up, ffn_conv_w, ffn_conv_b, ffn_down):
    D = D_MODEL
    c_act = jax.nn.silu(c)
    v_first = None
    for l in range(DEPTH):
        mod = (c_act @ w_mod[l] + b_mod[l])[:, None, :]
        sh1, sc1, gt1, sh2, sc2, gt2 = jnp.split(mod, 6, axis=-1)

        h = rmsnorm(x, norm1_g[l]) * (1 + sc1) + sh1
        w_cat = w_in[l] if l == 0 else jnp.concatenate([w_in[l], rwkv_v1[l - 1]], axis=1)
        z = h @ w_cat

        ya = rglru_branch(z[..., O_LX:O_LY], z[..., O_LY:O_Q], lru_conv_w[l], lru_conv_b[l],
                          lru_wa[l], lru_ba[l], lru_wx[l], lru_bx[l], lru_lambda[l])
        lambda_init = 0.8 - 0.6 * math.exp(-0.3 * l)
        yb = diff_attention(z[..., O_Q:O_K], z[..., O_K:O_V], z[..., O_V:O_RW], positions,
                            diff_lq1[l], diff_lk1[l], diff_lq2[l], diff_lk2[l],
                            diff_subln_g[l], lambda_init)
        if l == 0:
            yc, v_first = rwkv7_mix(z[..., O_RW:IN_COLS], rwkv_mu[l], rwkv_w0[l], rwkv_w2[l],
                                    rwkv_a0[l], rwkv_a2[l], rwkv_g2[l], rwkv_kk[l], rwkv_ka[l],
                                    rwkv_rk[l], rwkv_lnx_g[l], rwkv_lnx_b[l], None)
        else:
            yc, v_first = rwkv7_mix(z[..., O_RW:IN_COLS], rwkv_mu[l], rwkv_w0[l], rwkv_w2[l],
                                    rwkv_a0[l], rwkv_a2[l], rwkv_g2[l], rwkv_kk[l], rwkv_ka[l],
                                    rwkv_rk[l], rwkv_lnx_g[l], rwkv_lnx_b[l], v_first,
                                    zvd=z[..., IN_COLS:], v_mu=rwkv_vmu[l - 1],
                                    v0=rwkv_v0[l - 1], v2=rwkv_v2[l - 1])

        merged = (jax.nn.sigmoid(z[..., 0:D]) * (ya @ w_merge_a[l])
                  + jax.nn.sigmoid(z[..., D:2 * D]) * (yb @ w_merge_b[l])
                  + jax.nn.sigmoid(z[..., 2 * D:3 * D]) * (yc @ w_merge_c[l]))
        x = x + gt1 * (merged @ w_out[l])

        h2 = rmsnorm(x, norm2_g[l]) * (1 + sc2) + sh2
        x = x + gt2 * conv_gated_mlp(h2, ffn_up[l], ffn_conv_w[l], ffn_conv_b[l], ffn_down[l])
    return rmsnorm(x, final_g)
```

```python
import functools
import math

import jax
import jax.numpy as jnp
from jax import lax
from jax.experimental import pallas as pl
from jax.experimental.pallas import tpu as pltpu

F32 = jnp.float32
BF16 = jnp.bfloat16

D_MODEL = 1024
D_FF = 3 * D_MODEL
FFN_CONV = 3
LRU_WIDTH = D_MODEL // 2
LRU_BLOCKS = 8
LRU_BLOCK_DIM = LRU_WIDTH // LRU_BLOCKS
LRU_CONV = 4
LRU_C = 8.0
DIFF_HEADS = 4
DIFF_HEAD_DIM = D_MODEL // 16
DIFF_WIDTH = DIFF_HEADS * 2 * DIFF_HEAD_DIM
ROT_DIM = DIFF_HEAD_DIM // 4
ROPE_THETA = 500000.0
RWKV_HEAD_DIM = 64
RWKV_WIDTH = D_MODEL // 2
RWKV_HEADS = RWKV_WIDTH // RWKV_HEAD_DIM
DECAY_LORA = 64
AAA_LORA = 64
GATE_LORA = 128
MV_LORA = 32
RWKV_COLS = 3 * RWKV_WIDTH + DECAY_LORA + AAA_LORA + GATE_LORA
NORM_EPS = 1e-6
SUBLN_EPS = 1e-5
GN_EPS = 64e-5
N_BRANCH = 3
O_LX = N_BRANCH * D_MODEL
O_LY = O_LX + LRU_WIDTH
O_Q = O_LY + LRU_WIDTH
O_K = O_Q + DIFF_WIDTH
O_V = O_K + DIFF_WIDTH
O_RW = O_V + DIFF_WIDTH
IN_COLS = O_RW + RWKV_COLS
O_LORA = O_RW + 3 * RWKV_WIDTH

LANES = 128
SUBLANES = 8
IN_COLS_PAD = 7680
WKV_CHUNK = 64
WKV_GROUP = 4
VMEM_LIMIT = 48 * 1024 * 1024


def _cparams(sem):
    return pltpu.CompilerParams(dimension_semantics=sem, vmem_limit_bytes=VMEM_LIMIT)


def _dot(a, b):
    return jnp.dot(a, b, preferred_element_type=F32)


def _dot_nt(a, b):
    return lax.dot_general(a, b, (((1,), (1,)), ((), ())), preferred_element_type=F32)


def _dot_tn(a, b):
    return lax.dot_general(a, b, (((0,), (0,)), ((), ())), preferred_element_type=F32)


def _gelu_tanh(x):
    return 0.5 * x * (1.0 + jnp.tanh(math.sqrt(2.0 / math.pi) * (x + 0.044715 * (x * x * x))))


def _shift_rows(x, d, carry):
    xr = pltpu.roll(x, d, 0)
    cr = pltpu.roll(carry, d, 0)
    row8 = lax.broadcasted_iota(jnp.int32, carry.shape, 0)
    top = jnp.where(row8 < d, cr, xr[:SUBLANES])
    return jnp.concatenate([top, xr[SUBLANES:]], axis=0)


def _group_sum(x, bd):
    hi = x.astype(BF16)
    lo = (x - hi.astype(F32)).astype(BF16)
    return _dot(hi, bd) + _dot(lo, bd)


def _mod_kernel(c_ref, w_ref, b_ref, o_ref):
    c = c_ref[...]
    ca = c * jax.nn.sigmoid(c)
    o_ref[...] = jnp.dot(ca, w_ref[...], precision=lax.Precision.HIGHEST,
                         preferred_element_type=F32) + b_ref[...]


def _modulation(c, w_mod, b_mod):
    L, D, N = w_mod.shape
    B = c.shape[0]
    tn = 1536
    return pl.pallas_call(
        _mod_kernel,
        out_shape=jax.ShapeDtypeStruct((L, B, N), F32),
        grid=(L, N // tn),
        in_specs=[pl.BlockSpec((B, D), lambda l, j: (0, 0)),
                  pl.BlockSpec((None, D, tn), lambda l, j: (l, 0, j)),
                  pl.BlockSpec((None, 1, tn), lambda l, j: (l, 0, j))],
        out_specs=pl.BlockSpec((None, B, tn), lambda l, j: (l, 0, j)),
        compiler_params=_cparams(("arbitrary", "arbitrary")),
        name="adaln_modulation",
    )(c, w_mod, b_mod.reshape(L, 1, N))


def _inproj_kernel(x_ref, mod_ref, g_ref, w_ref, z_ref, h_ref):
    @pl.when(pl.program_id(1) == 0)
    def _():
        x = x_ref[...]
        ms = jnp.mean(x * x, axis=-1, keepdims=True)
        y = x * lax.rsqrt(ms + NORM_EPS) * g_ref[...]
        h_ref[...] = (y * (1.0 + mod_ref[1:2, :]) + mod_ref[0:1, :]).astype(BF16)

    z_ref[...] = _dot(h_ref[...], w_ref[...])


def _in_projection(x2, mod_l, g, w_cat, S):
    T, D = x2.shape
    N = w_cat.shape[1]
    tm = min(1024, S)
    tn = 1536
    per_b = S // tm
    return pl.pallas_call(
        _inproj_kernel,
        out_shape=jax.ShapeDtypeStruct((T, N), F32),
        grid=(T // tm, N // tn),
        in_specs=[pl.BlockSpec((tm, D), lambda i, j: (i, 0)),
                  pl.BlockSpec((None, 6, D), lambda i, j: (i // per_b, 0, 0)),
                  pl.BlockSpec((1, D), lambda i, j: (0, 0)),
                  pl.BlockSpec((D, tn), lambda i, j: (0, j))],
        out_specs=pl.BlockSpec((tm, tn), lambda i, j: (i, j)),
        scratch_shapes=[pltpu.VMEM((tm, D), BF16)],
        compiler_params=_cparams(("arbitrary", "arbitrary")),
        name="in_projection",
    )(x2, mod_l, g.reshape(1, D), w_cat)


def _lru_kernel(zx_ref, zy_ref, cw_ref, cb_ref, wa_ref, ba_ref, wx_ref, bx_ref, lam_ref,
                o_ref, xcarry_ref, hcarry_ref, *, ts):
    @pl.when(pl.program_id(1) == 0)
    def _():
        xcarry_ref[...] = jnp.zeros_like(xcarry_ref)
        hcarry_ref[...] = jnp.zeros_like(hcarry_ref)

    x = zx_ref[...]
    carry = xcarry_ref[...]
    xc = cb_ref[...] + cw_ref[LRU_CONV - 1:LRU_CONV, :] * x
    for d in range(1, LRU_CONV):
        xc = xc + cw_ref[LRU_CONV - 1 - d:LRU_CONV - d, :] * _shift_rows(x, d, carry)
    xcarry_ref[...] = x[ts - SUBLANES:, :]

    xb = xc.astype(BF16)
    r = jax.nn.sigmoid(_dot(xb, wa_ref[...]) + ba_ref[...])
    i = jax.nn.sigmoid(_dot(xb, wx_ref[...]) + bx_ref[...])
    nl = -lam_ref[...]
    softplus_nl = jnp.maximum(nl, 0.0) + jnp.log1p(jnp.exp(-jnp.abs(nl)))
    log_a = (-LRU_C) * r * softplus_nl
    a = jnp.exp(log_a)
    th = jnp.tanh(log_a)
    u = jnp.sqrt(-2.0 * th / (1.0 - th)) * (i * xc)

    row = lax.broadcasted_iota(jnp.int32, a.shape, 0)
    d = 1
    while d < ts:
        a_prev = jnp.where(row >= d, pltpu.roll(a, d, 0), 1.0)
        u_prev = jnp.where(row >= d, pltpu.roll(u, d, 0), 0.0)
        u = a * u_prev + u
        a = a * a_prev
        d *= 2
    h = u + a * hcarry_ref[...]
    hcarry_ref[...] = h[ts - 1:ts, :]
    o_ref[...] = h * _gelu_tanh(zy_ref[...])


def _rglru(z, S, conv_w, conv_b, wa_bd, ba, wx_bd, bx, lam):
    T = z.shape[0]
    W = LRU_WIDTH
    ts = min(512, S)
    per_b = S // ts
    B = T // S
    row = lambda a: a.reshape(1, W)
    const = lambda shape: pl.BlockSpec(shape, lambda b, s: (0, 0))
    return pl.pallas_call(
        functools.partial(_lru_kernel, ts=ts),
        out_shape=jax.ShapeDtypeStruct((T, W), F32),
        grid=(B, per_b),
        in_specs=[pl.BlockSpec((ts, W), lambda b, s: (b * per_b + s, O_LX // W)),
                  pl.BlockSpec((ts, W), lambda b, s: (b * per_b + s, O_LY // W)),
                  const((LRU_CONV, W)), const((1, W)), const((W, W)), const((1, W)),
                  const((W, W)), const((1, W)), const((1, W))],
        out_specs=pl.BlockSpec((ts, W), lambda b, s: (b * per_b + s, 0)),
        scratch_shapes=[pltpu.VMEM((SUBLANES, W), F32), pltpu.VMEM((1, W), F32)],
        compiler_params=_cparams(("arbitrary", "arbitrary")),
        name="rglru_scan",
    )(z, z, conv_w, row(conv_b), wa_bd, row(ba), wx_bd, row(bx), row(lam))


def _rope_table_kernel(pos_ref, invf_ref, ma_ref, mb_ref, cos_ref, sa_ref, sb_ref):
    ang = pos_ref[...] * invf_ref[...]
    s = jnp.sin(ang)
    cos_ref[...] = jnp.cos(ang)
    sa_ref[...] = s * ma_ref[...]
    sb_ref[...] = s * mb_ref[...]


def _rope_tables(positions):
    T = positions.size
    half = ROT_DIM // 2
    inv_freq = ROPE_THETA ** (-jnp.arange(0, ROT_DIM, 2, dtype=F32) / ROT_DIM)
    lane = jnp.arange(LANES)
    in_comp = lane % DIFF_HEAD_DIM
    invf = jnp.where(in_comp < ROT_DIM, inv_freq[in_comp % half], 0.0).reshape(1, LANES).astype(F32)
    ma = ((in_comp >= half) & (in_comp < ROT_DIM)).astype(F32).reshape(1, LANES)
    mb = -(in_comp < half).astype(F32).reshape(1, LANES)
    ts = min(2048, T)
    row = pl.BlockSpec((1, LANES), lambda i: (0, 0))
    tab = pl.BlockSpec((ts, LANES), lambda i: (i, 0))
    return pl.pallas_call(
        _rope_table_kernel,
        out_shape=[jax.ShapeDtypeStruct((T, LANES), F32)] * 3,
        grid=(T // ts,),
        in_specs=[pl.BlockSpec((ts, 1), lambda i: (i, 0)), row, row, row],
        out_specs=[tab, tab, tab],
        compiler_params=_cparams(("arbitrary",)),
        name="rope_tables",
    )(positions.astype(F32).reshape(T, 1), invf, ma, mb)


def _qkv_kernel(q_ref, k_ref, v_ref, cos_ref, sa_ref, sb_ref, qo_ref, ko_ref, vo_ref):
    c, sa, sb = cos_ref[...], sa_ref[...], sb_ref[...]
    half = ROT_DIM // 2
    for h in range(DIFF_HEADS):
        sl = slice(h * LANES, (h + 1) * LANES)
        for src, dst, scale in ((q_ref, qo_ref, DIFF_HEAD_DIM ** -0.5), (k_ref, ko_ref, None)):
            x = src[:, sl]
            y = x * c + pltpu.roll(x, half, 1) * sa + pltpu.roll(x, LANES - half, 1) * sb
            if scale is not None:
                y = y * scale
            dst[:, sl] = y.astype(BF16)
    vo_ref[...] = v_ref[...].astype(BF16)


def _qkv_prep(z, cos_t, sa_t, sb_t):
    T = z.shape[0]
    W = DIFF_WIDTH
    ts = min(512, T)
    zblk = lambda off: pl.BlockSpec((ts, W), lambda i: (i, off // W))
    tab = pl.BlockSpec((ts, LANES), lambda i: (i, 0))
    out = pl.BlockSpec((ts, W), lambda i: (i, 0))
    return pl.pallas_call(
        _qkv_kernel,
        out_shape=[jax.ShapeDtypeStruct((T, W), BF16)] * 3,
        grid=(T // ts,),
        in_specs=[zblk(O_Q), zblk(O_K), zblk(O_V), tab, tab, tab],
        out_specs=[out, out, out],
        compiler_params=_cparams(("arbitrary",)),
        name="qkv_rope_prep",
    )(z, z, z, cos_t, sa_t, sb_t)


def _flash_kernel(q_ref, k_ref, v_ref, lq1_ref, lk1_ref, lq2_ref, lk2_ref, sg_ref, o_ref,
                  m_ref, l_ref, acc_ref, *, tq, tk, nk, lambda_init):
    i = pl.program_id(2)
    j = pl.program_id(3)

    @pl.when(j == 0)
    def _():
        m_ref[...] = jnp.full_like(m_ref, -1e30)
        l_ref[...] = jnp.zeros_like(l_ref)
        acc_ref[...] = jnp.zeros_like(acc_ref)

    @pl.when(j * tk <= i * tq + tq - 1)
    def _():
        q = q_ref[...]
        k = k_ref[...]
        v = v_ref[...]
        lane = lax.broadcasted_iota(jnp.int32, q.shape, 1)
        zero = jnp.zeros_like(q)
        qidx = i * tq + lax.broadcasted_iota(jnp.int32, (tq, tk), 0)
        kidx = j * tk + lax.broadcasted_iota(jnp.int32, (tq, tk), 1)
        mask = kidx <= qidx
        for c in range(2):
            qc = jnp.where(lane < DIFF_HEAD_DIM, q, zero) if c == 0 else jnp.where(lane >= DIFF_HEAD_DIM, q, zero)
            s = jnp.where(mask, _dot_nt(qc, k), -1e30)
            m_prev = m_ref[c]
            m_new = jnp.maximum(m_prev, jnp.max(s, axis=-1, keepdims=True))
            alpha = jnp.exp(m_prev - m_new)
            p = jnp.exp(s - m_new)
            l_ref[c] = alpha * l_ref[c] + jnp.sum(p, axis=-1, keepdims=True)
            acc_ref[c] = alpha * acc_ref[c] + _dot(p.astype(BF16), v)
            m_ref[c] = m_new

    @pl.when(j == nk - 1)
    def _():
        lam = (jnp.exp(jnp.sum(lq1_ref[...] * lk1_ref[...], axis=-1, keepdims=True))
               - jnp.exp(jnp.sum(lq2_ref[...] * lk2_ref[...], axis=-1, keepdims=True)) + lambda_init)
        o = acc_ref[0] / l_ref[0] - lam * (acc_ref[1] / l_ref[1])
        ms = jnp.mean(o * o, axis=-1, keepdims=True)
        o_ref[...] = o * lax.rsqrt(ms + SUBLN_EPS) * sg_ref[...] * (1.0 - lambda_init)


def _diff_attention(qh, kh, vh, S, lq1, lk1, lq2, lk2, subln_g, lambda_init):
    T = qh.shape[0]
    B = T // S
    tq = tk = min(512, S)
    nq, nk = S // tq, S // tk
    hd = 2 * DIFF_HEAD_DIM
    vec = pl.BlockSpec((1, DIFF_HEAD_DIM), lambda b, h, i, j: (0, 0))
    kv = pl.BlockSpec((tk, hd), lambda b, h, i, j: (b * nk + jnp.minimum(j, (i * tq + tq - 1) // tk), h))
    return pl.pallas_call(
        functools.partial(_flash_kernel, tq=tq, tk=tk, nk=nk, lambda_init=lambda_init),
        out_shape=jax.ShapeDtypeStruct((T, DIFF_WIDTH), F32),
        grid=(B, DIFF_HEADS, nq, nk),
        in_specs=[pl.BlockSpec((tq, hd), lambda b, h, i, j: (b * nq + i, h)), kv, kv,
                  vec, vec, vec, vec, pl.BlockSpec((1, hd), lambda b, h, i, j: (0, 0))],
        out_specs=pl.BlockSpec((tq, hd), lambda b, h, i, j: (b * nq + i, h)),
        scratch_shapes=[pltpu.VMEM((2, tq, 1), F32), pltpu.VMEM((2, tq, 1), F32),
                        pltpu.VMEM((2, tq, hd), F32)],
        compiler_params=_cparams(("arbitrary",) * 4),
        name="diff_flash_attention",
    )(qh, kh, vh, lq1.reshape(1, -1), lk1.reshape(1, -1), lq2.reshape(1, -1), lk2.reshape(1, -1),
      subln_g.reshape(1, hd))


def _rwkv_prep_kernel(*refs, ts, has_vres):
    if has_vres:
        (zr_ref, zk_ref, zv_ref, zl_ref, zd_ref, vf_ref, mu_ref, mul_ref, w0_ref, w2_ref, a0_ref, a2_ref,
         g2_ref, kk_ref, ka_ref, bd_ref, vmu_ref, v0_ref, v2_ref,
         r_o, ld_o, k_o, v_o, kn_o, a_o, g_o, cr_ref, ck_ref, cv_ref, cl_ref, cd_ref) = refs
    else:
        (zr_ref, zk_ref, zv_ref, zl_ref, mu_ref, mul_ref, w0_ref, w2_ref, a0_ref, a2_ref,
         g2_ref, kk_ref, ka_ref, bd_ref,
         r_o, ld_o, k_o, v_o, kn_o, a_o, g_o, cr_ref, ck_ref, cv_ref, cl_ref) = refs

    @pl.when(pl.program_id(1) == 0)
    def _():
        for cref in (cr_ref, ck_ref, cv_ref, cl_ref) + ((cd_ref,) if has_vres else ()):
            cref[...] = jnp.zeros_like(cref)

    def mixed(z_ref, carry_ref, mu):
        z = z_ref[...]
        prev = _shift_rows(z, 1, carry_ref[...])
        carry_ref[...] = z[ts - SUBLANES:, :]
        return z + (prev - z) * mu

    W = RWKV_WIDTH
    r = mixed(zr_ref, cr_ref, mu_ref[:, 0:W])
    k = mixed(zk_ref, ck_ref, mu_ref[:, W:2 * W])
    v = mixed(zv_ref, cv_ref, mu_ref[:, 2 * W:3 * W])
    lora = mixed(zl_ref, cl_ref, mul_ref[...])
    wa_in = lora[:, 0:LANES]
    gd = lora[:, LANES:2 * LANES]

    xw = w0_ref[...] + _dot(jnp.tanh(wa_in).astype(BF16), w2_ref[...])
    ld_o[...] = -jax.nn.sigmoid(xw) * math.exp(-0.5)
    a = jax.nn.sigmoid(a0_ref[...] + _dot(wa_in.astype(BF16), a2_ref[...]))
    g_o[...] = _dot(jax.nn.sigmoid(gd).astype(BF16), g2_ref[...])
    if has_vres:
        zd = mixed(zd_ref, cd_ref, vmu_ref[...])
        v = v + (vf_ref[...] - v) * jax.nn.sigmoid(v0_ref[...] + _dot(zd.astype(BF16), v2_ref[...]))
    kk = k * kk_ref[...]
    ss = _group_sum(kk * kk, bd_ref[...])
    kn_o[...] = kk / jnp.maximum(jnp.sqrt(ss), 1e-12)
    k_o[...] = k * (1.0 + (a - 1.0) * ka_ref[...])
    r_o[...] = r
    v_o[...] = v
    a_o[...] = a


def _rwkv_prep(z, S, p, v_first):
    T = z.shape[0]
    B = T // S
    W = RWKV_WIDTH
    ts = min(512, S)
    per_b = S // ts
    has_vres = v_first is not None
    tok = lambda w, off: pl.BlockSpec((ts, w), lambda b, s: (b * per_b + s, off // w))
    const = lambda a: pl.BlockSpec(a.shape, lambda b, s: (0, 0))
    args = [z, z, z, z]
    specs = [tok(W, O_RW), tok(W, O_RW + W), tok(W, O_RW + 2 * W), tok(2 * LANES, O_LORA)]
    if has_vres:
        args += [z, v_first]
        specs += [tok(LANES, IN_COLS), tok(W, 0)]
    consts = [p["mu_rkv"], p["mu_lora"], p["w0"], p["w2"], p["a0"], p["a2"], p["g2"], p["kk"], p["ka"], p["bd"]]
    if has_vres:
        consts += [p["vmu"], p["v0"], p["v2"]]
    args += consts
    specs += [const(a) for a in consts]
    out = pl.BlockSpec((ts, W), lambda b, s: (b * per_b + s, 0))
    scratch = [pltpu.VMEM((SUBLANES, W), F32)] * 3 + [pltpu.VMEM((SUBLANES, 2 * LANES), F32)]
    if has_vres:
        scratch += [pltpu.VMEM((SUBLANES, LANES), F32)]
    return pl.pallas_call(
        functools.partial(_rwkv_prep_kernel, ts=ts, has_vres=has_vres),
        out_shape=[jax.ShapeDtypeStruct((T, W), F32)] * 7,
        grid=(B, per_b),
        in_specs=specs,
        out_specs=[out] * 7,
        scratch_shapes=scratch,
        compiler_params=_cparams(("arbitrary", "arbitrary")),
        name="rwkv7_prep",
    )(*args)


def _wkv_kernel(r_ref, ld_ref, k_ref, v_ref, kn_ref, a_ref, g_ref, rk_ref, lg_ref, lb_ref, bd_ref,
                o_ref, s_ref, *, n_chunks):
    C = WKV_CHUNK
    N = RWKV_HEAD_DIM
    GW = WKV_GROUP * N
    GR = WKV_GROUP * C
    CS = C.bit_length() - 1
    NS = N.bit_length() - 1

    @pl.when(pl.program_id(1) == 0)
    def _():
        s_ref[...] = jnp.zeros_like(s_ref)

    rowc = lax.broadcasted_iota(jnp.int32, (C, RWKV_WIDTH), 0)
    rr = lax.broadcasted_iota(jnp.int32, (GR, GW), 0)
    cc = lax.broadcasted_iota(jnp.int32, (GR, GW), 1)
    same_head = (rr >> CS) == (cc >> NS)
    rt = lax.broadcasted_iota(jnp.int32, (GR, GR), 0)
    ct = lax.broadcasted_iota(jnp.int32, (GR, GR), 1)
    m_strict = ((rt >> CS) == (ct >> CS)) & ((ct & (C - 1)) < (rt & (C - 1)))
    m_incl = ((rt >> CS) == (ct >> CS)) & ((ct & (C - 1)) <= (rt & (C - 1)))

    def expand(x):
        return jnp.where(same_head, jnp.concatenate([x] * WKV_GROUP, axis=0), 0.0).astype(BF16)

    def tiled(x):
        return jnp.concatenate([x] * WKV_GROUP, axis=0).astype(BF16)

    def chunk(ci, _):
        c0 = pl.multiple_of(ci * C, C)
        rows = pl.ds(c0, C)
        r, ld, k, v = r_ref[rows, :], ld_ref[rows, :], k_ref[rows, :], v_ref[rows, :]
        kn, a = kn_ref[rows, :], a_ref[rows, :]

        cum = ld
        d = 1
        while d < C:
            cum = cum + jnp.where(rowc >= d, pltpu.roll(cum, d, 0), 0.0)
            d *= 2
        tot = cum[C - 1:C, :]
        e_pos = jnp.exp(cum)
        e_neg = jnp.exp(-cum)
        e_end = jnp.exp(tot - cum)
        w_end = jnp.exp(tot)
        b = kn * a
        r_t = r * e_pos
        a_t = -kn * jnp.exp(cum - ld)
        b_t = b * e_neg
        k_t = k * e_neg
        b_e = b * e_end
        k_e = k * e_end

        pieces = []
        for gi in range(RWKV_HEADS // WKV_GROUP):
            sl = slice(gi * GW, (gi + 1) * GW)
            at_x, rt_x = expand(a_t[:, sl]), expand(r_t[:, sl])
            bt_t, kt_t = tiled(b_t[:, sl]), tiled(k_t[:, sl])
            vs = jnp.concatenate([v[:, (gi * WKV_GROUP + h) * N:(gi * WKV_GROUP + h + 1) * N]
                                  for h in range(WKV_GROUP)], axis=0)
            vs_b = vs.astype(BF16)
            state = s_ref[:, sl]
            st_b = state.astype(BF16)

            n_ab = jnp.where(m_strict, _dot_nt(at_x, bt_t), 0.0)
            n_ak = jnp.where(m_strict, _dot_nt(at_x, kt_t), 0.0)
            n_rb = jnp.where(m_incl, _dot_nt(rt_x, bt_t), 0.0)
            n_rk = jnp.where(m_incl, _dot_nt(rt_x, kt_t), 0.0)

            u = _dot_nt(at_x, st_b) + _dot(n_ak.astype(BF16), vs_b)
            pw = n_ab
            u = u + _dot(pw.astype(BF16), u.astype(BF16))
            step = 2
            while step < C:
                pb = pw.astype(BF16)
                pw = _dot(pb, pb)
                u = u + _dot(pw.astype(BF16), u.astype(BF16))
                step *= 2
            u_b = u.astype(BF16)

            o = _dot_nt(rt_x, st_b) + _dot(n_rb.astype(BF16), u_b) + _dot(n_rk.astype(BF16), vs_b)
            s_ref[:, sl] = (state * w_end[:, sl] + _dot_tn(u_b, expand(b_e[:, sl]))
                            + _dot_tn(vs_b, expand(k_e[:, sl])))

            mean = jnp.mean(o, axis=-1, keepdims=True)
            cen = o - mean
            var = jnp.mean(cen * cen, axis=-1, keepdims=True)
            on = cen * lax.rsqrt(var + GN_EPS)
            pieces += [on[h * C:(h + 1) * C, :] for h in range(WKV_GROUP)]

        on_wide = jnp.concatenate(pieces, axis=1)
        bonus = _group_sum(r * k * rk_ref[...], bd_ref[...]) * v
        o_ref[rows, :] = (on_wide * lg_ref[...] + lb_ref[...] + bonus) * g_ref[rows, :]
        return 0

    lax.fori_loop(0, n_chunks, chunk, 0)


def _wkv_scan(r, ld, k, v, kn, a, g, S, rk, lnx_g, lnx_b, bd):
    T, W = r.shape
    B = T // S
    tt = min(256, S)
    per_b = S // tt
    tok = pl.BlockSpec((tt, W), lambda b, s: (b * per_b + s, 0))
    rowspec = pl.BlockSpec((1, W), lambda b, s: (0, 0))
    return pl.pallas_call(
        functools.partial(_wkv_kernel, n_chunks=tt // WKV_CHUNK),
        out_shape=jax.ShapeDtypeStruct((T, W), F32),
        grid=(B, per_b),
        in_specs=[tok] * 7 + [rowspec] * 3 + [pl.BlockSpec((W, W), lambda b, s: (0, 0))],
        out_specs=tok,
        scratch_shapes=[pltpu.VMEM((RWKV_HEAD_DIM, W), F32)],
        compiler_params=_cparams(("arbitrary", "arbitrary")),
        name="rwkv7_chunk_scan",
    )(r, ld, k, v, kn, a, g, rk.reshape(1, W), lnx_g.reshape(1, W), lnx_b.reshape(1, W), bd)


def _merge_kernel(x_ref, mod_ref, ga_ref, gb_ref, gc_ref, ya_ref, yb_ref, yc_ref,
                  wa_ref, wb_ref, wc_ref, wo_ref, o_ref):
    merged = (jax.nn.sigmoid(ga_ref[...]) * _dot(ya_ref[...].astype(BF16), wa_ref[...])
              + jax.nn.sigmoid(gb_ref[...]) * _dot(yb_ref[...].astype(BF16), wb_ref[...])
              + jax.nn.sigmoid(gc_ref[...]) * _dot(yc_ref[...].astype(BF16), wc_ref[...]))
    o_ref[...] = x_ref[...] + mod_ref[2:3, :] * _dot(merged.astype(BF16), wo_ref[...])


def _merge(x2, mod_l, z, ya, yb, yc, wa, wb, wc, wo, S):
    T, D = x2.shape
    tm = min(512, S)
    per_b = S // tm
    tokd = pl.BlockSpec((tm, D), lambda i: (i, 0))
    gate = lambda n: pl.BlockSpec((tm, D), lambda i: (i, n))
    tokh = pl.BlockSpec((tm, D // 2), lambda i: (i, 0))
    wh = pl.BlockSpec((D // 2, D), lambda i: (0, 0))
    return pl.pallas_call(
        _merge_kernel,
        out_shape=jax.ShapeDtypeStruct((T, D), F32),
        grid=(T // tm,),
        in_specs=[tokd, pl.BlockSpec((None, 6, D), lambda i: (i // per_b, 0, 0)),
                  gate(0), gate(1), gate(2), tokh, tokh, tokh, wh, wh, wh,
                  pl.BlockSpec((D, D), lambda i: (0, 0))],
        out_specs=tokd,
        compiler_params=_cparams(("arbitrary",)),
        name="branch_merge_out_proj",
    )(x2, mod_l, z, z, z, ya, yb, yc, wa, wb, wc, wo)


def _ffn_kernel(x_ref, mod_ref, g_ref, wug_ref, wuv_ref, cwg_ref, cwv_ref, cbg_ref, cbv_ref, wd_ref, fg_ref,
                o_ref, h_ref, acc_ref, cg_ref, cv_ref, *, tm, per_b, nf, final):
    i = pl.program_id(0)
    j = pl.program_id(1)

    @pl.when(j == 0)
    def _():
        x = x_ref[...]
        ms = jnp.mean(x * x, axis=-1, keepdims=True)
        y = x * lax.rsqrt(ms + NORM_EPS) * g_ref[...]
        h_ref[...] = (y * (1.0 + mod_ref[4:5, :]) + mod_ref[3:4, :]).astype(BF16)
        acc_ref[...] = jnp.zeros_like(acc_ref)

    @pl.when((j == 0) & (i % per_b == 0))
    def _():
        cg_ref[...] = jnp.zeros_like(cg_ref)
        cv_ref[...] = jnp.zeros_like(cv_ref)

    hb = h_ref[...]

    def branch(wu_ref, cw_ref, cb_ref, carry_ref):
        u = _dot(hb, wu_ref[...])
        carry = carry_ref[j]
        out = cb_ref[...] + cw_ref[FFN_CONV - 1:FFN_CONV, :] * u
        for d in range(1, FFN_CONV):
            out = out + cw_ref[FFN_CONV - 1 - d:FFN_CONV - d, :] * _shift_rows(u, d, carry)
        carry_ref[j] = u[tm - SUBLANES:, :]
        return out

    gate = branch(wug_ref, cwg_ref, cbg_ref, cg_ref)
    val = branch(wuv_ref, cwv_ref, cbv_ref, cv_ref)
    acc_ref[...] += _dot((_gelu_tanh(gate) * val).astype(BF16), wd_ref[...])

    @pl.when(j == nf - 1)
    def _():
        out = x_ref[...] + mod_ref[5:6, :] * acc_ref[...]
        if final:
            ms = jnp.mean(out * out, axis=-1, keepdims=True)
            out = out * lax.rsqrt(ms + NORM_EPS) * fg_ref[...]
        o_ref[...] = out


def _ffn(x2, mod_l, g, w_up, conv_w, conv_b, w_down, final_g, S, final):
    T, D = x2.shape
    F = D_FF
    tm = min(512, S)
    tf = 512
    nf = F // tf
    per_b = S // tm
    conv_b = conv_b.reshape(1, 2 * F)
    return pl.pallas_call(
        functools.partial(_ffn_kernel, tm=tm, per_b=per_b, nf=nf, final=final),
        out_shape=jax.ShapeDtypeStruct((T, D), F32),
        grid=(T // tm, nf),
        in_specs=[pl.BlockSpec((tm, D), lambda i, j: (i, 0)),
                  pl.BlockSpec((None, 6, D), lambda i, j: (i // per_b, 0, 0)),
                  pl.BlockSpec((1, D), lambda i, j: (0, 0)),
                  pl.BlockSpec((D, tf), lambda i, j: (0, j)),
                  pl.BlockSpec((D, tf), lambda i, j: (0, nf + j)),
                  pl.BlockSpec((FFN_CONV, tf), lambda i, j: (0, j)),
                  pl.BlockSpec((FFN_CONV, tf), lambda i, j: (0, nf + j)),
                  pl.BlockSpec((1, tf), lambda i, j: (0, j)),
                  pl.BlockSpec((1, tf), lambda i, j: (0, nf + j)),
                  pl.BlockSpec((tf, D), lambda i, j: (j, 0)),
                  pl.BlockSpec((1, D), lambda i, j: (0, 0))],
        out_specs=pl.BlockSpec((tm, D), lambda i, j: (i, 0)),
        scratch_shapes=[pltpu.VMEM((tm, D), BF16), pltpu.VMEM((tm, D), F32),
                        pltpu.VMEM((nf, SUBLANES, tf), F32), pltpu.VMEM((nf, SUBLANES, tf), F32)],
        compiler_params=_cparams(("arbitrary", "arbitrary")),
        name="conv_gated_mlp",
    )(x2, mod_l, g.reshape(1, D), w_up, w_up, conv_w, conv_w, conv_b, conv_b, w_down, final_g.reshape(1, D))


def _block_diag(w):
    n, c, d = w.shape
    return jnp.einsum("ncd,nm->ncmd", w, jnp.eye(n, dtype=w.dtype)).reshape(n * c, n * d)


def _pad_rows(w, rows, offset):
    return jnp.zeros((rows, w.shape[1]), w.dtype).at[offset:offset + w.shape[0]].set(w)


def kernel(x, c, positions, w_mod, b_mod, norm1_g, norm2_g, final_g, w_in, w_merge_a, w_merge_b, w_merge_c, w_out, lru_conv_w, lru_conv_b, lru_wa, lru_ba, lru_wx, lru_bx, lru_lambda, diff_lq1, diff_lk1, diff_lq2, diff_lk2, diff_subln_g, rwkv_mu, rwkv_w0, rwkv_w2, rwkv_a0, rwkv_a2, rwkv_g2, rwkv_kk, rwkv_ka, rwkv_rk, rwkv_lnx_g, rwkv_lnx_b, rwkv_v0, rwkv_v1, rwkv_v2, rwkv_vmu, ffn_up, ffn_conv_w, ffn_conv_b, ffn_down):
    B, S, D = x.shape
    L = w_in.shape[0]
    T = B * S
    W = RWKV_WIDTH
    x2 = x.reshape(T, D)

    mod = _modulation(c, w_mod, b_mod).reshape(L, B, 6, D)
    cos_t, sa_t, sb_t = _rope_tables(positions)
    head_ones = _block_diag(jnp.ones((RWKV_HEADS, RWKV_HEAD_DIM, RWKV_HEAD_DIM), BF16))
    row = lambda a: a.reshape(1, -1)

    v_first = None
    for l in range(L):
        w_cat = w_in[l] if l == 0 else jnp.concatenate([w_in[l], rwkv_v1[l - 1]], axis=1)
        w_cat = jnp.pad(w_cat, ((0, 0), (0, IN_COLS_PAD - w_cat.shape[1]))).astype(BF16)
        z = _in_projection(x2, mod[l], norm1_g[l], w_cat, S)

        ya = _rglru(z, S, lru_conv_w[l], lru_conv_b[l], _block_diag(lru_wa[l]).astype(BF16), lru_ba[l],
                    _block_diag(lru_wx[l]).astype(BF16), lru_bx[l], lru_lambda[l])

        qh, kh, vh = _qkv_prep(z, cos_t, sa_t, sb_t)
        lambda_init = 0.8 - 0.6 * math.exp(-0.3 * l)
        yb = _diff_attention(qh, kh, vh, S, diff_lq1[l], diff_lk1[l], diff_lq2[l], diff_lk2[l],
                             diff_subln_g[l], lambda_init)

        mu = rwkv_mu[l]
        prm = {
            "mu_rkv": row(mu[:3 * W]), "mu_lora": row(mu[3 * W:]),
            "w0": row(rwkv_w0[l]), "w2": _pad_rows(rwkv_w2[l], LANES, 0).astype(BF16),
            "a0": row(rwkv_a0[l]), "a2": _pad_rows(rwkv_a2[l], LANES, DECAY_LORA).astype(BF16),
            "g2": rwkv_g2[l].astype(BF16), "kk": row(rwkv_kk[l]), "ka": row(rwkv_ka[l]), "bd": head_ones,
        }
        if l > 0:
            prm["vmu"] = jnp.pad(row(rwkv_vmu[l - 1]), ((0, 0), (0, LANES - MV_LORA)))
            prm["v0"] = row(rwkv_v0[l - 1])
            prm["v2"] = _pad_rows(rwkv_v2[l - 1], LANES, 0).astype(BF16)
        r, ld, k, v, kn, a, g = _rwkv_prep(z, S, prm, v_first)
        if l == 0:
            v_first = v
        yc = _wkv_scan(r, ld, k, v, kn, a, g, S, rwkv_rk[l], rwkv_lnx_g[l], rwkv_lnx_b[l], head_ones)

        x2 = _merge(x2, mod[l], z, ya, yb, yc, w_merge_a[l].astype(BF16), w_merge_b[l].astype(BF16),
                    w_merge_c[l].astype(BF16), w_out[l].astype(BF16), S)
        x2 = _ffn(x2, mod[l], norm2_g[l], ffn_up[l].astype(BF16), ffn_conv_w[l], ffn_conv_b[l],
                  ffn_down[l].astype(BF16), final_g, S, final=(l == L - 1))
    return x2.reshape(B, S, D)
```

```python
import functools
import math

import jax
import jax.numpy as jnp
from jax import lax
from jax.experimental import pallas as pl
from jax.experimental.pallas import tpu as pltpu

F32 = jnp.float32
BF16 = jnp.bfloat16

D_MODEL = 1024
D_FF = 3 * D_MODEL
FFN_CONV = 3
LRU_WIDTH = D_MODEL // 2
LRU_BLOCKS = 8
LRU_BLOCK_DIM = LRU_WIDTH // LRU_BLOCKS
LRU_CONV = 4
LRU_C = 8.0
DIFF_HEADS = 4
DIFF_HEAD_DIM = D_MODEL // 16
DIFF_WIDTH = DIFF_HEADS * 2 * DIFF_HEAD_DIM
ROT_DIM = DIFF_HEAD_DIM // 4
ROPE_THETA = 500000.0
RWKV_HEAD_DIM = 64
RWKV_WIDTH = D_MODEL // 2
RWKV_HEADS = RWKV_WIDTH // RWKV_HEAD_DIM
DECAY_LORA = 64
AAA_LORA = 64
GATE_LORA = 128
MV_LORA = 32
RWKV_COLS = 3 * RWKV_WIDTH + DECAY_LORA + AAA_LORA + GATE_LORA
NORM_EPS = 1e-6
SUBLN_EPS = 1e-5
GN_EPS = 64e-5
N_BRANCH = 3
O_LX = N_BRANCH * D_MODEL
O_LY = O_LX + LRU_WIDTH
O_Q = O_LY + LRU_WIDTH
O_K = O_Q + DIFF_WIDTH
O_V = O_K + DIFF_WIDTH
O_RW = O_V + DIFF_WIDTH
IN_COLS = O_RW + RWKV_COLS
O_LORA = O_RW + 3 * RWKV_WIDTH

LOG2_E = math.log2(math.e)
LANES = 128
SUBLANES = 8
IN_COLS_PAD = 7680
WKV_CHUNK = 64
WKV_GROUP = 4
VMEM_LIMIT = 48 * 1024 * 1024


def _cparams(sem):
    return pltpu.CompilerParams(dimension_semantics=sem, vmem_limit_bytes=VMEM_LIMIT)


def _dot(a, b):
    return jnp.dot(a, b, preferred_element_type=F32)


def _dot_nt(a, b):
    return lax.dot_general(a, b, (((1,), (1,)), ((), ())), preferred_element_type=F32)


def _dot_tn(a, b):
    return lax.dot_general(a, b, (((0,), (0,)), ((), ())), preferred_element_type=F32)


def _gelu_tanh(x):
    return 0.5 * x * (1.0 + jnp.tanh(math.sqrt(2.0 / math.pi) * (x + 0.044715 * (x * x * x))))


def _shift_rows(x, d, carry):
    xr = pltpu.roll(x, d, 0)
    cr = pltpu.roll(carry, d, 0)
    row8 = lax.broadcasted_iota(jnp.int32, carry.shape, 0)
    top = jnp.where(row8 < d, cr, xr[:SUBLANES])
    return jnp.concatenate([top, xr[SUBLANES:]], axis=0)


def _group_sum(x, bd):
    hi = x.astype(BF16)
    lo = (x - hi.astype(F32)).astype(BF16)
    return _dot(hi, bd) + _dot(lo, bd)


def _mod_kernel(c_ref, w_ref, b_ref, o_ref):
    c = c_ref[...]
    ca = c * jax.nn.sigmoid(c)
    o_ref[...] = jnp.dot(ca, w_ref[...], precision=lax.Precision.HIGHEST,
                         preferred_element_type=F32) + b_ref[...]


def _modulation(c, w_mod, b_mod):
    L, D, N = w_mod.shape
    B = c.shape[0]
    tn = 1536
    return pl.pallas_call(
        _mod_kernel,
        out_shape=jax.ShapeDtypeStruct((L, B, N), F32),
        grid=(L, N // tn),
        in_specs=[pl.BlockSpec((B, D), lambda l, j: (0, 0)),
                  pl.BlockSpec((None, D, tn), lambda l, j: (l, 0, j)),
                  pl.BlockSpec((None, 1, tn), lambda l, j: (l, 0, j))],
        out_specs=pl.BlockSpec((None, B, tn), lambda l, j: (l, 0, j)),
        compiler_params=_cparams(("arbitrary", "arbitrary")),
        name="adaln_modulation",
    )(c, w_mod, b_mod.reshape(L, 1, N))


def _inproj_kernel(x_ref, mod_ref, g_ref, w_ref, z_ref, h_ref):
    @pl.when(pl.program_id(1) == 0)
    def _():
        x = x_ref[...]
        ms = jnp.mean(x * x, axis=-1, keepdims=True)
        y = x * lax.rsqrt(ms + NORM_EPS) * g_ref[...]
        h_ref[...] = (y * (1.0 + mod_ref[1:2, :]) + mod_ref[0:1, :]).astype(BF16)

    z_ref[...] = _dot(h_ref[...], w_ref[...])


def _in_projection(x2, mod_l, g, w_cat, S):
    T, D = x2.shape
    N = w_cat.shape[1]
    tm = min(1024, S)
    tn = 1536
    per_b = S // tm
    return pl.pallas_call(
        _inproj_kernel,
        out_shape=jax.ShapeDtypeStruct((T, N), F32),
        grid=(T // tm, N // tn),
        in_specs=[pl.BlockSpec((tm, D), lambda i, j: (i, 0)),
                  pl.BlockSpec((None, 6, D), lambda i, j: (i // per_b, 0, 0)),
                  pl.BlockSpec((1, D), lambda i, j: (0, 0)),
                  pl.BlockSpec((D, tn), lambda i, j: (0, j))],
        out_specs=pl.BlockSpec((tm, tn), lambda i, j: (i, j)),
        scratch_shapes=[pltpu.VMEM((tm, D), BF16)],
        compiler_params=_cparams(("arbitrary", "arbitrary")),
        name="in_projection",
    )(x2, mod_l, g.reshape(1, D), w_cat)


def _lru_kernel(zx_ref, zy_ref, cw_ref, cb_ref, wa_ref, ba_ref, wx_ref, bx_ref, lam_ref,
                o_ref, xcarry_ref, hcarry_ref, *, ts):
    @pl.when(pl.program_id(1) == 0)
    def _():
        xcarry_ref[...] = jnp.zeros_like(xcarry_ref)
        hcarry_ref[...] = jnp.zeros_like(hcarry_ref)

    x = zx_ref[...]
    carry = xcarry_ref[...]
    xc = cb_ref[...] + cw_ref[LRU_CONV - 1:LRU_CONV, :] * x
    for d in range(1, LRU_CONV):
        xc = xc + cw_ref[LRU_CONV - 1 - d:LRU_CONV - d, :] * _shift_rows(x, d, carry)
    xcarry_ref[...] = x[ts - SUBLANES:, :]

    xb = xc.astype(BF16)
    r = jax.nn.sigmoid(_dot(xb, wa_ref[...]) + ba_ref[...])
    i = jax.nn.sigmoid(_dot(xb, wx_ref[...]) + bx_ref[...])
    nl = -lam_ref[...]
    softplus_nl = jnp.maximum(nl, 0.0) + jnp.log1p(jnp.exp(-jnp.abs(nl)))
    log_a = (-LRU_C) * r * softplus_nl
    a = jnp.exp(log_a)
    th = jnp.tanh(log_a)
    u = jnp.sqrt(-2.0 * th / (1.0 - th)) * (i * xc)

    row = lax.broadcasted_iota(jnp.int32, a.shape, 0)
    d = 1
    while d < ts:
        a_prev = jnp.where(row >= d, pltpu.roll(a, d, 0), 1.0)
        u_prev = jnp.where(row >= d, pltpu.roll(u, d, 0), 0.0)
        u = a * u_prev + u
        a = a * a_prev
        d *= 2
    h = u + a * hcarry_ref[...]
    hcarry_ref[...] = h[ts - 1:ts, :]
    o_ref[...] = h * _gelu_tanh(zy_ref[...])


def _rglru(z, S, conv_w, conv_b, wa_bd, ba, wx_bd, bx, lam):
    T = z.shape[0]
    W = LRU_WIDTH
    ts = min(512, S)
    per_b = S // ts
    B = T // S
    row = lambda a: a.reshape(1, W)
    const = lambda shape: pl.BlockSpec(shape, lambda b, s: (0, 0))
    return pl.pallas_call(
        functools.partial(_lru_kernel, ts=ts),
        out_shape=jax.ShapeDtypeStruct((T, W), F32),
        grid=(B, per_b),
        in_specs=[pl.BlockSpec((ts, W), lambda b, s: (b * per_b + s, O_LX // W)),
                  pl.BlockSpec((ts, W), lambda b, s: (b * per_b + s, O_LY // W)),
                  const((LRU_CONV, W)), const((1, W)), const((W, W)), const((1, W)),
                  const((W, W)), const((1, W)), const((1, W))],
        out_specs=pl.BlockSpec((ts, W), lambda b, s: (b * per_b + s, 0)),
        scratch_shapes=[pltpu.VMEM((SUBLANES, W), F32), pltpu.VMEM((1, W), F32)],
        compiler_params=_cparams(("arbitrary", "arbitrary")),
        name="rglru_scan",
    )(z, z, conv_w, row(conv_b), wa_bd, row(ba), wx_bd, row(bx), row(lam))


def _rope_table_kernel(pos_ref, invf_ref, ma_ref, mb_ref, cos_ref, sa_ref, sb_ref):
    ang = pos_ref[...] * invf_ref[...]
    s = jnp.sin(ang)
    cos_ref[...] = jnp.cos(ang)
    sa_ref[...] = s * ma_ref[...]
    sb_ref[...] = s * mb_ref[...]


def _rope_tables(positions):
    T = positions.size
    half = ROT_DIM // 2
    inv_freq = ROPE_THETA ** (-jnp.arange(0, ROT_DIM, 2, dtype=F32) / ROT_DIM)
    lane = jnp.arange(LANES)
    in_comp = lane % DIFF_HEAD_DIM
    invf = jnp.where(in_comp < ROT_DIM, inv_freq[in_comp % half], 0.0).reshape(1, LANES).astype(F32)
    ma = ((in_comp >= half) & (in_comp < ROT_DIM)).astype(F32).reshape(1, LANES)
    mb = -(in_comp < half).astype(F32).reshape(1, LANES)
    ts = min(2048, T)
    row = pl.BlockSpec((1, LANES), lambda i: (0, 0))
    tab = pl.BlockSpec((ts, LANES), lambda i: (i, 0))
    return pl.pallas_call(
        _rope_table_kernel,
        out_shape=[jax.ShapeDtypeStruct((T, LANES), F32)] * 3,
        grid=(T // ts,),
        in_specs=[pl.BlockSpec((ts, 1), lambda i: (i, 0)), row, row, row],
        out_specs=[tab, tab, tab],
        compiler_params=_cparams(("arbitrary",)),
        name="rope_tables",
    )(positions.astype(F32).reshape(T, 1), invf, ma, mb)


def _qkv_kernel(q_ref, k_ref, v_ref, cos_ref, sa_ref, sb_ref, qo_ref, ko_ref, vo_ref):
    c, sa, sb = cos_ref[...], sa_ref[...], sb_ref[...]
    half = ROT_DIM // 2
    for h in range(DIFF_HEADS):
        sl = slice(h * LANES, (h + 1) * LANES)
        for src, dst, scale in ((q_ref, qo_ref, DIFF_HEAD_DIM ** -0.5 * LOG2_E), (k_ref, ko_ref, None)):
            x = src[:, sl]
            y = x * c + pltpu.roll(x, half, 1) * sa + pltpu.roll(x, LANES - half, 1) * sb
            if scale is not None:
                y = y * scale
            dst[:, sl] = y.astype(BF16)
    vo_ref[...] = v_ref[...].T.astype(BF16)


def _qkv_prep(z, cos_t, sa_t, sb_t, S):
    T = z.shape[0]
    W = DIFF_WIDTH
    ts = min(512, S)
    per_b = S // ts
    zblk = lambda off: pl.BlockSpec((ts, W), lambda i: (i, off // W))
    tab = pl.BlockSpec((ts, LANES), lambda i: (i, 0))
    out = pl.BlockSpec((ts, W), lambda i: (i, 0))
    return pl.pallas_call(
        _qkv_kernel,
        out_shape=[jax.ShapeDtypeStruct((T, W), BF16)] * 2 + [jax.ShapeDtypeStruct((T // S, W, S), BF16)],
        grid=(T // ts,),
        in_specs=[zblk(O_Q), zblk(O_K), zblk(O_V), tab, tab, tab],
        out_specs=[out, out, pl.BlockSpec((None, W, ts), lambda i: (i // per_b, 0, i % per_b))],
        compiler_params=_cparams(("arbitrary",)),
        name="qkv_rope_prep",
    )(z, z, z, cos_t, sa_t, sb_t)


def _flash_kernel(it_ref, jt_ref, q_ref, k_ref, vt_ref, lq1_ref, lk1_ref, lq2_ref, lk2_ref, sg_ref, o_ref,
                  m_ref, l_ref, acc_ref, *, t, lambda_init):
    p = pl.program_id(2)
    i = it_ref[p]
    j = jt_ref[p]

    @pl.when(j == 0)
    def _():
        m_ref[...] = jnp.full_like(m_ref, -1e30)
        l_ref[...] = jnp.zeros_like(l_ref)
        acc_ref[...] = jnp.zeros_like(acc_ref)

    def step(diagonal):
        q = q_ref[...]
        k = k_ref[...]
        vt = vt_ref[...]
        lane = lax.broadcasted_iota(jnp.int32, q.shape, 1)
        zero = jnp.zeros_like(q)
        if diagonal:
            visible = (lax.broadcasted_iota(jnp.int32, (t, t), 0) <= lax.broadcasted_iota(jnp.int32, (t, t), 1))
        for c in range(2):
            qc = jnp.where(lane < DIFF_HEAD_DIM, q, zero) if c == 0 else jnp.where(lane >= DIFF_HEAD_DIM, q, zero)
            s = _dot_nt(k, qc)
            if diagonal:
                s = jnp.where(visible, s, -1e30)
            m_prev = m_ref[c]
            m_new = jnp.maximum(m_prev, jnp.max(s, axis=0, keepdims=True))
            alpha = jnp.exp2(m_prev - m_new)
            pr = jnp.exp2(s - m_new)
            l_ref[c] = alpha * l_ref[c] + jnp.sum(pr, axis=0, keepdims=True)
            acc_ref[c] = alpha * acc_ref[c] + _dot(vt, pr.astype(BF16))
            m_ref[c] = m_new

    @pl.when(j < i)
    def _():
        step(False)

    @pl.when(j == i)
    def _():
        step(True)
        lam = (jnp.exp(jnp.sum(lq1_ref[...] * lk1_ref[...], axis=-1, keepdims=True))
               - jnp.exp(jnp.sum(lq2_ref[...] * lk2_ref[...], axis=-1, keepdims=True)) + lambda_init)
        o = acc_ref[0] / l_ref[0] - lam * (acc_ref[1] / l_ref[1])
        ms = jnp.mean(o * o, axis=0, keepdims=True)
        o_ref[...] = (o * lax.rsqrt(ms + SUBLN_EPS) * sg_ref[...] * (1.0 - lambda_init)).T


def _diff_attention(qh, kh, vt, S, lq1, lk1, lq2, lk2, subln_g, lambda_init):
    T = qh.shape[0]
    B = T // S
    t = min(512, S)
    n = S // t
    hd = 2 * DIFF_HEAD_DIM
    pairs = [(i, j) for i in range(n) for j in range(i + 1)]
    i_tab = jnp.asarray([p[0] for p in pairs], jnp.int32)
    j_tab = jnp.asarray([p[1] for p in pairs], jnp.int32)
    vec = pl.BlockSpec((1, DIFF_HEAD_DIM), lambda b, h, p, it, jt: (0, 0))
    grid_spec = pltpu.PrefetchScalarGridSpec(
        num_scalar_prefetch=2,
        grid=(B, DIFF_HEADS, len(pairs)),
        in_specs=[pl.BlockSpec((t, hd), lambda b, h, p, it, jt: (b * n + it[p], h)),
                  pl.BlockSpec((t, hd), lambda b, h, p, it, jt: (b * n + jt[p], h)),
                  pl.BlockSpec((None, hd, t), lambda b, h, p, it, jt: (b, h, jt[p])),
                  vec, vec, vec, vec, pl.BlockSpec((hd, 1), lambda b, h, p, it, jt: (0, 0))],
        out_specs=pl.BlockSpec((t, hd), lambda b, h, p, it, jt: (b * n + it[p], h)),
        scratch_shapes=[pltpu.VMEM((2, 1, t), F32), pltpu.VMEM((2, 1, t), F32), pltpu.VMEM((2, hd, t), F32)],
    )
    return pl.pallas_call(
        functools.partial(_flash_kernel, t=t, lambda_init=lambda_init),
        out_shape=jax.ShapeDtypeStruct((T, DIFF_WIDTH), F32),
        grid_spec=grid_spec,
        compiler_params=_cparams(("arbitrary",) * 3),
        name="diff_flash_attention",
    )(i_tab, j_tab, qh, kh, vt, lq1.reshape(1, -1), lk1.reshape(1, -1), lq2.reshape(1, -1), lk2.reshape(1, -1),
      subln_g.reshape(hd, 1))


def _rwkv_prep_kernel(*refs, ts, has_vres):
    if has_vres:
        (zr_ref, zk_ref, zv_ref, zl_ref, zd_ref, vf_ref, mu_ref, mul_ref, w0_ref, w2_ref, a0_ref, a2_ref,
         g2_ref, kk_ref, ka_ref, bd_ref, vmu_ref, v0_ref, v2_ref,
         r_o, ld_o, k_o, v_o, kn_o, a_o, g_o, cr_ref, ck_ref, cv_ref, cl_ref, cd_ref) = refs
    else:
        (zr_ref, zk_ref, zv_ref, zl_ref, mu_ref, mul_ref, w0_ref, w2_ref, a0_ref, a2_ref,
         g2_ref, kk_ref, ka_ref, bd_ref,
         r_o, ld_o, k_o, v_o, kn_o, a_o, g_o, cr_ref, ck_ref, cv_ref, cl_ref) = refs

    @pl.when(pl.program_id(1) == 0)
    def _():
        for cref in (cr_ref, ck_ref, cv_ref, cl_ref) + ((cd_ref,) if has_vres else ()):
            cref[...] = jnp.zeros_like(cref)

    def mixed(z_ref, carry_ref, mu):
        z = z_ref[...]
        prev = _shift_rows(z, 1, carry_ref[...])
        carry_ref[...] = z[ts - SUBLANES:, :]
        return z + (prev - z) * mu

    W = RWKV_WIDTH
    r = mixed(zr_ref, cr_ref, mu_ref[:, 0:W])
    k = mixed(zk_ref, ck_ref, mu_ref[:, W:2 * W])
    v = mixed(zv_ref, cv_ref, mu_ref[:, 2 * W:3 * W])
    lora = mixed(zl_ref, cl_ref, mul_ref[...])
    wa_in = lora[:, 0:LANES]
    gd = lora[:, LANES:2 * LANES]

    xw = w0_ref[...] + _dot(jnp.tanh(wa_in).astype(BF16), w2_ref[...])
    ld_o[...] = -jax.nn.sigmoid(xw) * math.exp(-0.5)
    a = jax.nn.sigmoid(a0_ref[...] + _dot(wa_in.astype(BF16), a2_ref[...]))
    g_o[...] = _dot(jax.nn.sigmoid(gd).astype(BF16), g2_ref[...])
    if has_vres:
        zd = mixed(zd_ref, cd_ref, vmu_ref[...])
        v = v + (vf_ref[...] - v) * jax.nn.sigmoid(v0_ref[...] + _dot(zd.astype(BF16), v2_ref[...]))
    kk = k * kk_ref[...]
    ss = _group_sum(kk * kk, bd_ref[...])
    kn_o[...] = kk / jnp.maximum(jnp.sqrt(ss), 1e-12)
    k_o[...] = k * (1.0 + (a - 1.0) * ka_ref[...])
    r_o[...] = r
    v_o[...] = v
    a_o[...] = a


def _rwkv_prep(z, S, p, v_first):
    T = z.shape[0]
    B = T // S
    W = RWKV_WIDTH
    ts = min(512, S)
    per_b = S // ts
    has_vres = v_first is not None
    tok = lambda w, off: pl.BlockSpec((ts, w), lambda b, s: (b * per_b + s, off // w))
    const = lambda a: pl.BlockSpec(a.shape, lambda b, s: (0, 0))
    args = [z, z, z, z]
    specs = [tok(W, O_RW), tok(W, O_RW + W), tok(W, O_RW + 2 * W), tok(2 * LANES, O_LORA)]
    if has_vres:
        args += [z, v_first]
        specs += [tok(LANES, IN_COLS), tok(W, 0)]
    consts = [p["mu_rkv"], p["mu_lora"], p["w0"], p["w2"], p["a0"], p["a2"], p["g2"], p["kk"], p["ka"], p["bd"]]
    if has_vres:
        consts += [p["vmu"], p["v0"], p["v2"]]
    args += consts
    specs += [const(a) for a in consts]
    out = pl.BlockSpec((ts, W), lambda b, s: (b * per_b + s, 0))
    scratch = [pltpu.VMEM((SUBLANES, W), F32)] * 3 + [pltpu.VMEM((SUBLANES, 2 * LANES), F32)]
    if has_vres:
        scratch += [pltpu.VMEM((SUBLANES, LANES), F32)]
    return pl.pallas_call(
        functools.partial(_rwkv_prep_kernel, ts=ts, has_vres=has_vres),
        out_shape=[jax.ShapeDtypeStruct((T, W), F32)] * 7,
        grid=(B, per_b),
        in_specs=specs,
        out_specs=[out] * 7,
        scratch_shapes=scratch,
        compiler_params=_cparams(("arbitrary", "arbitrary")),
        name="rwkv7_prep",
    )(*args)


def _wkv_kernel(r_ref, ld_ref, k_ref, v_ref, kn_ref, a_ref, g_ref, rk_ref, lg_ref, lb_ref, bd_ref,
                o_ref, s_ref, *, n_chunks, nb):
    C = WKV_CHUNK
    N = RWKV_HEAD_DIM
    GW = WKV_GROUP * N
    GR = WKV_GROUP * C
    CS = C.bit_length() - 1
    NS = N.bit_length() - 1

    @pl.when(pl.program_id(1) == 0)
    def _():
        s_ref[...] = jnp.zeros_like(s_ref)

    rowc = lax.broadcasted_iota(jnp.int32, (C, RWKV_WIDTH), 0)
    rr = lax.broadcasted_iota(jnp.int32, (GR, GW), 0)
    cc = lax.broadcasted_iota(jnp.int32, (GR, GW), 1)
    same_head = (rr >> CS) == (cc >> NS)
    rt = lax.broadcasted_iota(jnp.int32, (GR, GR), 0)
    ct = lax.broadcasted_iota(jnp.int32, (GR, GR), 1)
    m_strict = ((rt >> CS) == (ct >> CS)) & ((ct & (C - 1)) < (rt & (C - 1)))
    m_incl = ((rt >> CS) == (ct >> CS)) & ((ct & (C - 1)) <= (rt & (C - 1)))

    def expand(x):
        return jnp.where(same_head, jnp.concatenate([x] * WKV_GROUP, axis=0), 0.0).astype(BF16)

    def tiled(x):
        return jnp.concatenate([x] * WKV_GROUP, axis=0).astype(BF16)

    def one_chunk(bi, rows):
        r, ld, k, v = r_ref[bi, rows, :], ld_ref[bi, rows, :], k_ref[bi, rows, :], v_ref[bi, rows, :]
        kn, a = kn_ref[bi, rows, :], a_ref[bi, rows, :]

        cum = ld
        d = 1
        while d < C:
            cum = cum + jnp.where(rowc >= d, pltpu.roll(cum, d, 0), 0.0)
            d *= 2
        tot = cum[C - 1:C, :]
        e_pos = jnp.exp(cum)
        e_neg = jnp.exp(-cum)
        e_end = jnp.exp(tot - cum)
        w_end = jnp.exp(tot)
        b = kn * a
        r_t = r * e_pos
        a_t = -kn * jnp.exp(cum - ld)
        b_t = b * e_neg
        k_t = k * e_neg
        b_e = b * e_end
        k_e = k * e_end

        pieces = []
        for gi in range(RWKV_HEADS // WKV_GROUP):
            sl = slice(gi * GW, (gi + 1) * GW)
            at_x, rt_x = expand(a_t[:, sl]), expand(r_t[:, sl])
            bt_t, kt_t = tiled(b_t[:, sl]), tiled(k_t[:, sl])
            vs = jnp.concatenate([v[:, (gi * WKV_GROUP + h) * N:(gi * WKV_GROUP + h + 1) * N]
                                  for h in range(WKV_GROUP)], axis=0)
            vs_b = vs.astype(BF16)
            state = s_ref[bi, :, sl]
            st_b = state.astype(BF16)

            n_ab = jnp.where(m_strict, _dot_nt(at_x, bt_t), 0.0)
            n_ak = jnp.where(m_strict, _dot_nt(at_x, kt_t), 0.0)
            n_rb = jnp.where(m_incl, _dot_nt(rt_x, bt_t), 0.0)
            n_rk = jnp.where(m_incl, _dot_nt(rt_x, kt_t), 0.0)

            u = _dot_nt(at_x, st_b) + _dot(n_ak.astype(BF16), vs_b)
            pw = n_ab
            u = u + _dot(pw.astype(BF16), u.astype(BF16))
            step = 2
            while step < C:
                pb = pw.astype(BF16)
                pw = _dot(pb, pb)
                u = u + _dot(pw.astype(BF16), u.astype(BF16))
                step *= 2
            u_b = u.astype(BF16)

            o = _dot_nt(rt_x, st_b) + _dot(n_rb.astype(BF16), u_b) + _dot(n_rk.astype(BF16), vs_b)
            s_ref[bi, :, sl] = (state * w_end[:, sl] + _dot_tn(u_b, expand(b_e[:, sl]))
                                + _dot_tn(vs_b, expand(k_e[:, sl])))

            mean = jnp.mean(o, axis=-1, keepdims=True)
            cen = o - mean
            var = jnp.mean(cen * cen, axis=-1, keepdims=True)
            on = cen * lax.rsqrt(var + GN_EPS)
            pieces += [on[h * C:(h + 1) * C, :] for h in range(WKV_GROUP)]

        on_wide = jnp.concatenate(pieces, axis=1)
        bonus = _group_sum(r * k * rk_ref[...], bd_ref[...]) * v
        o_ref[bi, rows, :] = (on_wide * lg_ref[...] + lb_ref[...] + bonus) * g_ref[bi, rows, :]

    def chunk(ci, _):
        rows = pl.ds(pl.multiple_of(ci * C, C), C)
        for bi in range(nb):
            one_chunk(bi, rows)
        return 0

    lax.fori_loop(0, n_chunks, chunk, 0)


def _wkv_scan(r, ld, k, v, kn, a, g, S, rk, lnx_g, lnx_b, bd):
    T, W = r.shape
    B = T // S
    nb = 2 if B % 2 == 0 else 1
    tt = min(256, S)
    seq = lambda x: x.reshape(B, S, W)
    tok = pl.BlockSpec((nb, tt, W), lambda b, s: (b, s, 0))
    rowspec = pl.BlockSpec((1, W), lambda b, s: (0, 0))
    out = pl.pallas_call(
        functools.partial(_wkv_kernel, n_chunks=tt // WKV_CHUNK, nb=nb),
        out_shape=jax.ShapeDtypeStruct((B, S, W), F32),
        grid=(B // nb, S // tt),
        in_specs=[tok] * 7 + [rowspec] * 3 + [pl.BlockSpec((W, W), lambda b, s: (0, 0))],
        out_specs=tok,
        scratch_shapes=[pltpu.VMEM((nb, RWKV_HEAD_DIM, W), F32)],
        compiler_params=_cparams(("arbitrary", "arbitrary")),
        name="rwkv7_chunk_scan",
    )(seq(r), seq(ld), seq(k), seq(v), seq(kn), seq(a), seq(g),
      rk.reshape(1, W), lnx_g.reshape(1, W), lnx_b.reshape(1, W), bd)
    return out.reshape(T, W)


def _merge_kernel(x_ref, mod_ref, ga_ref, gb_ref, gc_ref, ya_ref, yb_ref, yc_ref,
                  wa_ref, wb_ref, wc_ref, wo_ref, o_ref):
    merged = (jax.nn.sigmoid(ga_ref[...]) * _dot(ya_ref[...].astype(BF16), wa_ref[...])
              + jax.nn.sigmoid(gb_ref[...]) * _dot(yb_ref[...].astype(BF16), wb_ref[...])
              + jax.nn.sigmoid(gc_ref[...]) * _dot(yc_ref[...].astype(BF16), wc_ref[...]))
    o_ref[...] = x_ref[...] + mod_ref[2:3, :] * _dot(merged.astype(BF16), wo_ref[...])


def _merge(x2, mod_l, z, ya, yb, yc, wa, wb, wc, wo, S):
    T, D = x2.shape
    tm = min(512, S)
    per_b = S // tm
    tokd = pl.BlockSpec((tm, D), lambda i: (i, 0))
    gate = lambda n: pl.BlockSpec((tm, D), lambda i: (i, n))
    tokh = pl.BlockSpec((tm, D // 2), lambda i: (i, 0))
    wh = pl.BlockSpec((D // 2, D), lambda i: (0, 0))
    return pl.pallas_call(
        _merge_kernel,
        out_shape=jax.ShapeDtypeStruct((T, D), F32),
        grid=(T // tm,),
        in_specs=[tokd, pl.BlockSpec((None, 6, D), lambda i: (i // per_b, 0, 0)),
                  gate(0), gate(1), gate(2), tokh, tokh, tokh, wh, wh, wh,
                  pl.BlockSpec((D, D), lambda i: (0, 0))],
        out_specs=tokd,
        compiler_params=_cparams(("arbitrary",)),
        name="branch_merge_out_proj",
    )(x2, mod_l, z, z, z, ya, yb, yc, wa, wb, wc, wo)


def _ffn_kernel(x_ref, mod_ref, g_ref, wug_ref, wuv_ref, cwg_ref, cwv_ref, cbg_ref, cbv_ref, wd_ref, fg_ref,
                o_ref, h_ref, acc_ref, cg_ref, cv_ref, *, tm, per_b, nf, final):
    i = pl.program_id(0)
    j = pl.program_id(1)

    @pl.when(j == 0)
    def _():
        x = x_ref[...]
        ms = jnp.mean(x * x, axis=-1, keepdims=True)
        y = x * lax.rsqrt(ms + NORM_EPS) * g_ref[...]
        h_ref[...] = (y * (1.0 + mod_ref[4:5, :]) + mod_ref[3:4, :]).astype(BF16)
        acc_ref[...] = jnp.zeros_like(acc_ref)

    @pl.when((j == 0) & (i % per_b == 0))
    def _():
        cg_ref[...] = jnp.zeros_like(cg_ref)
        cv_ref[...] = jnp.zeros_like(cv_ref)

    hb = h_ref[...]

    def branch(wu_ref, cw_ref, cb_ref, carry_ref):
        u = _dot(hb, wu_ref[...])
        carry = carry_ref[j]
        out = cb_ref[...] + cw_ref[FFN_CONV - 1:FFN_CONV, :] * u
        for d in range(1, FFN_CONV):
            out = out + cw_ref[FFN_CONV - 1 - d:FFN_CONV - d, :] * _shift_rows(u, d, carry)
        carry_ref[j] = u[tm - SUBLANES:, :]
        return out

    gate = branch(wug_ref, cwg_ref, cbg_ref, cg_ref)
    val = branch(wuv_ref, cwv_ref, cbv_ref, cv_ref)
    acc_ref[...] += _dot((_gelu_tanh(gate) * val).astype(BF16), wd_ref[...])

    @pl.when(j == nf - 1)
    def _():
        out = x_ref[...] + mod_ref[5:6, :] * acc_ref[...]
        if final:
            ms = jnp.mean(out * out, axis=-1, keepdims=True)
            out = out * lax.rsqrt(ms + NORM_EPS) * fg_ref[...]
        o_ref[...] = out


def _ffn(x2, mod_l, g, w_up, conv_w, conv_b, w_down, final_g, S, final):
    T, D = x2.shape
    F = D_FF
    tm = min(512, S)
    tf = 512
    nf = F // tf
    per_b = S // tm
    conv_b = conv_b.reshape(1, 2 * F)
    return pl.pallas_call(
        functools.partial(_ffn_kernel, tm=tm, per_b=per_b, nf=nf, final=final),
        out_shape=jax.ShapeDtypeStruct((T, D), F32),
        grid=(T // tm, nf),
        in_specs=[pl.BlockSpec((tm, D), lambda i, j: (i, 0)),
                  pl.BlockSpec((None, 6, D), lambda i, j: (i // per_b, 0, 0)),
                  pl.BlockSpec((1, D), lambda i, j: (0, 0)),
                  pl.BlockSpec((D, tf), lambda i, j: (0, j)),
                  pl.BlockSpec((D, tf), lambda i, j: (0, nf + j)),
                  pl.BlockSpec((FFN_CONV, tf), lambda i, j: (0, j)),
                  pl.BlockSpec((FFN_CONV, tf), lambda i, j: (0, nf + j)),
                  pl.BlockSpec((1, tf), lambda i, j: (0, j)),
                  pl.BlockSpec((1, tf), lambda i, j: (0, nf + j)),
                  pl.BlockSpec((tf, D), lambda i, j: (j, 0)),
                  pl.BlockSpec((1, D), lambda i, j: (0, 0))],
        out_specs=pl.BlockSpec((tm, D), lambda i, j: (i, 0)),
        scratch_shapes=[pltpu.VMEM((tm, D), BF16), pltpu.VMEM((tm, D), F32),
                        pltpu.VMEM((nf, SUBLANES, tf), F32), pltpu.VMEM((nf, SUBLANES, tf), F32)],
        compiler_params=_cparams(("arbitrary", "arbitrary")),
        name="conv_gated_mlp",
    )(x2, mod_l, g.reshape(1, D), w_up, w_up, conv_w, conv_w, conv_b, conv_b, w_down, final_g.reshape(1, D))


def _block_diag(w):
    n, c, d = w.shape
    return jnp.einsum("ncd,nm->ncmd", w, jnp.eye(n, dtype=w.dtype)).reshape(n * c, n * d)


def _pad_rows(w, rows, offset):
    return jnp.zeros((rows, w.shape[1]), w.dtype).at[offset:offset + w.shape[0]].set(w)


def kernel(x, c, positions, w_mod, b_mod, norm1_g, norm2_g, final_g, w_in, w_merge_a, w_merge_b, w_merge_c, w_out, lru_conv_w, lru_conv_b, lru_wa, lru_ba, lru_wx, lru_bx, lru_lambda, diff_lq1, diff_lk1, diff_lq2, diff_lk2, diff_subln_g, rwkv_mu, rwkv_w0, rwkv_w2, rwkv_a0, rwkv_a2, rwkv_g2, rwkv_kk, rwkv_ka, rwkv_rk, rwkv_lnx_g, rwkv_lnx_b, rwkv_v0, rwkv_v1, rwkv_v2, rwkv_vmu, ffn_up, ffn_conv_w, ffn_conv_b, ffn_down):
    B, S, D = x.shape
    L = w_in.shape[0]
    T = B * S
    W = RWKV_WIDTH
    x2 = x.reshape(T, D)

    mod = _modulation(c, w_mod, b_mod).reshape(L, B, 6, D)
    cos_t, sa_t, sb_t = _rope_tables(positions)
    head_ones = _block_diag(jnp.ones((RWKV_HEADS, RWKV_HEAD_DIM, RWKV_HEAD_DIM), BF16))
    row = lambda a: a.reshape(1, -1)

    v_first = None
    for l in range(L):
        w_cat = w_in[l] if l == 0 else jnp.concatenate([w_in[l], rwkv_v1[l - 1]], axis=1)
        w_cat = jnp.pad(w_cat, ((0, 0), (0, IN_COLS_PAD - w_cat.shape[1]))).astype(BF16)
        z = _in_projection(x2, mod[l], norm1_g[l], w_cat, S)

        ya = _rglru(z, S, lru_conv_w[l], lru_conv_b[l], _block_diag(lru_wa[l]).astype(BF16), lru_ba[l],
                    _block_diag(lru_wx[l]).astype(BF16), lru_bx[l], lru_lambda[l])

        qh, kh, vt = _qkv_prep(z, cos_t, sa_t, sb_t, S)
        lambda_init = 0.8 - 0.6 * math.exp(-0.3 * l)
        yb = _diff_attention(qh, kh, vt, S, diff_lq1[l], diff_lk1[l], diff_lq2[l], diff_lk2[l],
                             diff_subln_g[l], lambda_init)

        mu = rwkv_mu[l]
        prm = {
            "mu_rkv": row(mu[:3 * W]), "mu_lora": row(mu[3 * W:]),
            "w0": row(rwkv_w0[l]), "w2": _pad_rows(rwkv_w2[l], LANES, 0).astype(BF16),
            "a0": row(rwkv_a0[l]), "a2": _pad_rows(rwkv_a2[l], LANES, DECAY_LORA).astype(BF16),
            "g2": rwkv_g2[l].astype(BF16), "kk": row(rwkv_kk[l]), "ka": row(rwkv_ka[l]), "bd": head_ones,
        }
        if l > 0:
            prm["vmu"] = jnp.pad(row(rwkv_vmu[l - 1]), ((0, 0), (0, LANES - MV_LORA)))
            prm["v0"] = row(rwkv_v0[l - 1])
            prm["v2"] = _pad_rows(rwkv_v2[l - 1], LANES, 0).astype(BF16)
        r, ld, k, v, kn, a, g = _rwkv_prep(z, S, prm, v_first)
        if l == 0:
            v_first = v
        yc = _wkv_scan(r, ld, k, v, kn, a, g, S, rwkv_rk[l], rwkv_lnx_g[l], rwkv_lnx_b[l], head_ones)

        x2 = _merge(x2, mod[l], z, ya, yb, yc, w_merge_a[l].astype(BF16), w_merge_b[l].astype(BF16),
                    w_merge_c[l].astype(BF16), w_out[l].astype(BF16), S)
        x2 = _ffn(x2, mod[l], norm2_g[l], ffn_up[l].astype(BF16), ffn_conv_w[l], ffn_conv_b[l],
                  ffn_down[l].astype(BF16), final_g, S, final=(l == L - 1))
    return x2.reshape(B, S, D)
```

```python
import functools
import math

import jax
import jax.numpy as jnp
from jax import lax
from jax.experimental import pallas as pl
from jax.experimental.pallas import tpu as pltpu

F32 = jnp.float32
BF16 = jnp.bfloat16

D_MODEL = 1024
D_FF = 3 * D_MODEL
FFN_CONV = 3
LRU_WIDTH = D_MODEL // 2
LRU_BLOCKS = 8
LRU_BLOCK_DIM = LRU_WIDTH // LRU_BLOCKS
LRU_CONV = 4
LRU_C = 8.0
DIFF_HEADS = 4
DIFF_HEAD_DIM = D_MODEL // 16
DIFF_WIDTH = DIFF_HEADS * 2 * DIFF_HEAD_DIM
ROT_DIM = DIFF_HEAD_DIM // 4
ROPE_THETA = 500000.0
RWKV_HEAD_DIM = 64
RWKV_WIDTH = D_MODEL // 2
RWKV_HEADS = RWKV_WIDTH // RWKV_HEAD_DIM
DECAY_LORA = 64
AAA_LORA = 64
GATE_LORA = 128
MV_LORA = 32
RWKV_COLS = 3 * RWKV_WIDTH + DECAY_LORA + AAA_LORA + GATE_LORA
NORM_EPS = 1e-6
SUBLN_EPS = 1e-5
GN_EPS = 64e-5
N_BRANCH = 3
O_LX = N_BRANCH * D_MODEL
O_LY = O_LX + LRU_WIDTH
O_Q = O_LY + LRU_WIDTH
O_K = O_Q + DIFF_WIDTH
O_V = O_K + DIFF_WIDTH
O_RW = O_V + DIFF_WIDTH
IN_COLS = O_RW + RWKV_COLS
O_LORA = O_RW + 3 * RWKV_WIDTH

LOG2_E = math.log2(math.e)
LANES = 128
SUBLANES = 8
IN_COLS_PAD = 7680
WKV_CHUNK = 64
WKV_GROUP = 4
VMEM_LIMIT = 48 * 1024 * 1024


def _cparams(sem):
    return pltpu.CompilerParams(dimension_semantics=sem, vmem_limit_bytes=VMEM_LIMIT)


def _dot(a, b):
    return jnp.dot(a, b, preferred_element_type=F32)


def _dot_nt(a, b):
    return lax.dot_general(a, b, (((1,), (1,)), ((), ())), preferred_element_type=F32)


def _dot_tn(a, b):
    return lax.dot_general(a, b, (((0,), (0,)), ((), ())), preferred_element_type=F32)


def _gelu_tanh(x):
    return 0.5 * x * (1.0 + jnp.tanh(math.sqrt(2.0 / math.pi) * (x + 0.044715 * (x * x * x))))


def _shift_rows(x, d, carry):
    xr = pltpu.roll(x, d, 0)
    cr = pltpu.roll(carry, d, 0)
    row8 = lax.broadcasted_iota(jnp.int32, carry.shape, 0)
    top = jnp.where(row8 < d, cr, xr[:SUBLANES])
    return jnp.concatenate([top, xr[SUBLANES:]], axis=0)


def _group_sum(x, bd):
    hi = x.astype(BF16)
    lo = (x - hi.astype(F32)).astype(BF16)
    return _dot(hi, bd) + _dot(lo, bd)


def _mod_kernel(c_ref, w_ref, b_ref, o_ref):
    c = c_ref[...]
    ca = c * jax.nn.sigmoid(c)
    o_ref[...] = jnp.dot(ca, w_ref[...], precision=lax.Precision.HIGHEST,
                         preferred_element_type=F32) + b_ref[...]


def _modulation(c, w_mod, b_mod):
    L, D, N = w_mod.shape
    B = c.shape[0]
    tn = 1536
    return pl.pallas_call(
        _mod_kernel,
        out_shape=jax.ShapeDtypeStruct((L, B, N), F32),
        grid=(L, N // tn),
        in_specs=[pl.BlockSpec((B, D), lambda l, j: (0, 0)),
                  pl.BlockSpec((None, D, tn), lambda l, j: (l, 0, j)),
                  pl.BlockSpec((None, 1, tn), lambda l, j: (l, 0, j))],
        out_specs=pl.BlockSpec((None, B, tn), lambda l, j: (l, 0, j)),
        compiler_params=_cparams(("arbitrary", "arbitrary")),
        name="adaln_modulation",
    )(c, w_mod, b_mod.reshape(L, 1, N))


def _inproj_kernel(x_ref, mod_ref, g_ref, w_ref, z_ref, h_ref):
    @pl.when(pl.program_id(1) == 0)
    def _():
        x = x_ref[...]
        ms = jnp.mean(x * x, axis=-1, keepdims=True)
        y = x * lax.rsqrt(ms + NORM_EPS) * g_ref[...]
        h_ref[...] = (y * (1.0 + mod_ref[1:2, :]) + mod_ref[0:1, :]).astype(BF16)

    z_ref[...] = _dot(h_ref[...], w_ref[...])


def _in_projection(x2, mod_l, g, w_cat, S):
    T, D = x2.shape
    N = w_cat.shape[1]
    tm = min(2048, S)
    tn = 768
    per_b = S // tm
    return pl.pallas_call(
        _inproj_kernel,
        out_shape=jax.ShapeDtypeStruct((T, N), F32),
        grid=(T // tm, N // tn),
        in_specs=[pl.BlockSpec((tm, D), lambda i, j: (i, 0)),
                  pl.BlockSpec((None, 6, D), lambda i, j: (i // per_b, 0, 0)),
                  pl.BlockSpec((1, D), lambda i, j: (0, 0)),
                  pl.BlockSpec((D, tn), lambda i, j: (0, j))],
        out_specs=pl.BlockSpec((tm, tn), lambda i, j: (i, j)),
        scratch_shapes=[pltpu.VMEM((tm, D), BF16)],
        compiler_params=_cparams(("arbitrary", "arbitrary")),
        name="in_projection",
    )(x2, mod_l, g.reshape(1, D), w_cat)


def _lru_kernel(zx_ref, zy_ref, cw_ref, cb_ref, wa_ref, ba_ref, wx_ref, bx_ref, lam_ref,
                o_ref, xcarry_ref, hcarry_ref, *, ts):
    @pl.when(pl.program_id(1) == 0)
    def _():
        xcarry_ref[...] = jnp.zeros_like(xcarry_ref)
        hcarry_ref[...] = jnp.zeros_like(hcarry_ref)

    x = zx_ref[...]
    carry = xcarry_ref[...]
    xc = cb_ref[...] + cw_ref[LRU_CONV - 1:LRU_CONV, :] * x
    for d in range(1, LRU_CONV):
        xc = xc + cw_ref[LRU_CONV - 1 - d:LRU_CONV - d, :] * _shift_rows(x, d, carry)
    xcarry_ref[...] = x[ts - SUBLANES:, :]

    xb = xc.astype(BF16)
    r = jax.nn.sigmoid(_dot(xb, wa_ref[...]) + ba_ref[...])
    i = jax.nn.sigmoid(_dot(xb, wx_ref[...]) + bx_ref[...])
    nl = -lam_ref[...]
    softplus_nl = jnp.maximum(nl, 0.0) + jnp.log1p(jnp.exp(-jnp.abs(nl)))
    log_a = (-LRU_C) * r * softplus_nl
    a = jnp.exp(log_a)
    th = jnp.tanh(log_a)
    u = jnp.sqrt(-2.0 * th / (1.0 - th)) * (i * xc)

    sub = lax.broadcasted_iota(jnp.int32, a.shape, 0) & (SUBLANES - 1)
    d = 1
    while d < SUBLANES:
        a_prev = jnp.where(sub >= d, pltpu.roll(a, d, 0), 1.0)
        u_prev = jnp.where(sub >= d, pltpu.roll(u, d, 0), 0.0)
        u = a * u_prev + u
        a = a * a_prev
        d *= 2
    carry = hcarry_ref[...]
    groups = []
    for g0 in range(0, ts, SUBLANES):
        hg = u[g0:g0 + SUBLANES, :] + a[g0:g0 + SUBLANES, :] * carry
        carry = hg[SUBLANES - 1:SUBLANES, :]
        groups.append(hg)
    hcarry_ref[...] = carry
    o_ref[...] = jnp.concatenate(groups, axis=0) * _gelu_tanh(zy_ref[...])


def _rglru(z, S, conv_w, conv_b, wa_bd, ba, wx_bd, bx, lam):
    T = z.shape[0]
    W = LRU_WIDTH
    ts = min(512, S)
    per_b = S // ts
    B = T // S
    row = lambda a: a.reshape(1, W)
    const = lambda shape: pl.BlockSpec(shape, lambda b, s: (0, 0))
    return pl.pallas_call(
        functools.partial(_lru_kernel, ts=ts),
        out_shape=jax.ShapeDtypeStruct((T, W), F32),
        grid=(B, per_b),
        in_specs=[pl.BlockSpec((ts, W), lambda b, s: (b * per_b + s, O_LX // W)),
                  pl.BlockSpec((ts, W), lambda b, s: (b * per_b + s, O_LY // W)),
                  const((LRU_CONV, W)), const((1, W)), const((W, W)), const((1, W)),
                  const((W, W)), const((1, W)), const((1, W))],
        out_specs=pl.BlockSpec((ts, W), lambda b, s: (b * per_b + s, 0)),
        scratch_shapes=[pltpu.VMEM((SUBLANES, W), F32), pltpu.VMEM((1, W), F32)],
        compiler_params=_cparams(("arbitrary", "arbitrary")),
        name="rglru_scan",
    )(z, z, conv_w, row(conv_b), wa_bd, row(ba), wx_bd, row(bx), row(lam))


def _rope_table_kernel(pos_ref, invf_ref, ma_ref, mb_ref, cos_ref, sa_ref, sb_ref):
    ang = pos_ref[...] * invf_ref[...]
    s = jnp.sin(ang)
    cos_ref[...] = jnp.cos(ang)
    sa_ref[...] = s * ma_ref[...]
    sb_ref[...] = s * mb_ref[...]


def _rope_tables(positions):
    T = positions.size
    half = ROT_DIM // 2
    inv_freq = ROPE_THETA ** (-jnp.arange(0, ROT_DIM, 2, dtype=F32) / ROT_DIM)
    lane = jnp.arange(LANES)
    in_comp = lane % DIFF_HEAD_DIM
    invf = jnp.where(in_comp < ROT_DIM, inv_freq[in_comp % half], 0.0).reshape(1, LANES).astype(F32)
    ma = ((in_comp >= half) & (in_comp < ROT_DIM)).astype(F32).reshape(1, LANES)
    mb = -(in_comp < half).astype(F32).reshape(1, LANES)
    ts = min(2048, T)
    row = pl.BlockSpec((1, LANES), lambda i: (0, 0))
    tab = pl.BlockSpec((ts, LANES), lambda i: (i, 0))
    return pl.pallas_call(
        _rope_table_kernel,
        out_shape=[jax.ShapeDtypeStruct((T, LANES), F32)] * 3,
        grid=(T // ts,),
        in_specs=[pl.BlockSpec((ts, 1), lambda i: (i, 0)), row, row, row],
        out_specs=[tab, tab, tab],
        compiler_params=_cparams(("arbitrary",)),
        name="rope_tables",
    )(positions.astype(F32).reshape(T, 1), invf, ma, mb)


def _qkv_kernel(q_ref, k_ref, v_ref, cos_ref, sa_ref, sb_ref, qo_ref, ko_ref, vo_ref):
    c, sa, sb = cos_ref[...], sa_ref[...], sb_ref[...]
    half = ROT_DIM // 2
    for h in range(DIFF_HEADS):
        sl = slice(h * LANES, (h + 1) * LANES)
        for src, dst, scale in ((q_ref, qo_ref, DIFF_HEAD_DIM ** -0.5 * LOG2_E), (k_ref, ko_ref, None)):
            x = src[:, sl]
            y = x * c + pltpu.roll(x, half, 1) * sa + pltpu.roll(x, LANES - half, 1) * sb
            if scale is not None:
                y = y * scale
            dst[:, sl] = y.astype(BF16)
    vo_ref[...] = v_ref[...].T.astype(BF16)


def _qkv_prep(z, cos_t, sa_t, sb_t, S):
    T = z.shape[0]
    W = DIFF_WIDTH
    ts = min(512, S)
    per_b = S // ts
    zblk = lambda off: pl.BlockSpec((ts, W), lambda i: (i, off // W))
    tab = pl.BlockSpec((ts, LANES), lambda i: (i, 0))
    out = pl.BlockSpec((ts, W), lambda i: (i, 0))
    return pl.pallas_call(
        _qkv_kernel,
        out_shape=[jax.ShapeDtypeStruct((T, W), BF16)] * 2 + [jax.ShapeDtypeStruct((T // S, W, S), BF16)],
        grid=(T // ts,),
        in_specs=[zblk(O_Q), zblk(O_K), zblk(O_V), tab, tab, tab],
        out_specs=[out, out, pl.BlockSpec((None, W, ts), lambda i: (i // per_b, 0, i % per_b))],
        compiler_params=_cparams(("arbitrary",)),
        name="qkv_rope_prep",
    )(z, z, z, cos_t, sa_t, sb_t)


def _flash_kernel(it_ref, jt_ref, q_ref, k_ref, vt_ref, lq1_ref, lk1_ref, lq2_ref, lk2_ref, sg_ref, o_ref,
                  m_ref, l_ref, acc_ref, *, t, lambda_init):
    p = pl.program_id(1)
    i = it_ref[p]
    j = jt_ref[p]
    hd = 2 * DIFF_HEAD_DIM

    @pl.when(j == 0)
    def _():
        m_ref[...] = jnp.full_like(m_ref, -1e30)
        l_ref[...] = jnp.zeros_like(l_ref)
        acc_ref[...] = jnp.zeros_like(acc_ref)

    def step(diagonal):
        lane = lax.broadcasted_iota(jnp.int32, (t, hd), 1)
        first = lane < DIFF_HEAD_DIM
        if diagonal:
            visible = (lax.broadcasted_iota(jnp.int32, (t, t), 0) <= lax.broadcasted_iota(jnp.int32, (t, t), 1))
        for h in range(DIFF_HEADS):
            hs = slice(h * hd, (h + 1) * hd)
            q = q_ref[:, hs]
            k = k_ref[:, hs]
            vt = vt_ref[hs, :]
            zero = jnp.zeros_like(k)
            k2 = jnp.concatenate([jnp.where(first, k, zero), jnp.where(first, zero, k)], axis=0)
            s2 = _dot_nt(k2, q)
            for c in range(2):
                s = s2[c * t:(c + 1) * t]
                if diagonal:
                    s = jnp.where(visible, s, -1e30)
                idx = 2 * h + c
                m_prev = m_ref[idx]
                m_new = jnp.maximum(m_prev, jnp.max(s, axis=0, keepdims=True))
                alpha = jnp.exp2(m_prev - m_new)
                pr = jnp.exp2(s - m_new)
                l_ref[idx] = alpha * l_ref[idx] + jnp.sum(pr, axis=0, keepdims=True)
                acc_ref[idx] = alpha * acc_ref[idx] + _dot(vt, pr.astype(BF16))
                m_ref[idx] = m_new

    @pl.when(j < i)
    def _():
        step(False)

    @pl.when(j == i)
    def _():
        step(True)
        lam = (jnp.exp(jnp.sum(lq1_ref[...] * lk1_ref[...], axis=-1, keepdims=True))
               - jnp.exp(jnp.sum(lq2_ref[...] * lk2_ref[...], axis=-1, keepdims=True)) + lambda_init)
        outs = []
        for h in range(DIFF_HEADS):
            o = acc_ref[2 * h] / l_ref[2 * h] - lam * (acc_ref[2 * h + 1] / l_ref[2 * h + 1])
            ms = jnp.mean(o * o, axis=0, keepdims=True)
            outs.append((o * lax.rsqrt(ms + SUBLN_EPS) * sg_ref[...] * (1.0 - lambda_init)).T)
        o_ref[...] = jnp.concatenate(outs, axis=1)


def _diff_attention(qh, kh, vt, S, lq1, lk1, lq2, lk2, subln_g, lambda_init):
    T, W = qh.shape
    B = T // S
    t = min(512, S)
    n = S // t
    hd = 2 * DIFF_HEAD_DIM
    pairs = [(i, j) for i in range(n) for j in range(i + 1)]
    i_tab = jnp.asarray([p[0] for p in pairs], jnp.int32)
    j_tab = jnp.asarray([p[1] for p in pairs], jnp.int32)
    vec = pl.BlockSpec((1, DIFF_HEAD_DIM), lambda b, p, it, jt: (0, 0))
    grid_spec = pltpu.PrefetchScalarGridSpec(
        num_scalar_prefetch=2,
        grid=(B, len(pairs)),
        in_specs=[pl.BlockSpec((t, W), lambda b, p, it, jt: (b * n + it[p], 0)),
                  pl.BlockSpec((t, W), lambda b, p, it, jt: (b * n + jt[p], 0)),
                  pl.BlockSpec((None, W, t), lambda b, p, it, jt: (b, 0, jt[p])),
                  vec, vec, vec, vec, pl.BlockSpec((hd, 1), lambda b, p, it, jt: (0, 0))],
        out_specs=pl.BlockSpec((t, W), lambda b, p, it, jt: (b * n + it[p], 0)),
        scratch_shapes=[pltpu.VMEM((2 * DIFF_HEADS, 1, t), F32), pltpu.VMEM((2 * DIFF_HEADS, 1, t), F32),
                        pltpu.VMEM((2 * DIFF_HEADS, hd, t), F32)],
    )
    return pl.pallas_call(
        functools.partial(_flash_kernel, t=t, lambda_init=lambda_init),
        out_shape=jax.ShapeDtypeStruct((T, W), F32),
        grid_spec=grid_spec,
        compiler_params=_cparams(("arbitrary",) * 2),
        name="diff_flash_attention",
    )(i_tab, j_tab, qh, kh, vt, lq1.reshape(1, -1), lk1.reshape(1, -1), lq2.reshape(1, -1), lk2.reshape(1, -1),
      subln_g.reshape(hd, 1))


def _rwkv_prep_kernel(*refs, ts, has_vres):
    if has_vres:
        (zr_ref, zk_ref, zv_ref, zl_ref, zd_ref, vf_ref, mu_ref, mul_ref, w0_ref, w2_ref, a0_ref, a2_ref,
         g2_ref, kk_ref, ka_ref, rk_ref, bd_ref, vmu_ref, v0_ref, v2_ref,
         r_o, ld_o, k_o, v_o, kn_o, a_o, g_o, bon_o, cr_ref, ck_ref, cv_ref, cl_ref, cd_ref) = refs
    else:
        (zr_ref, zk_ref, zv_ref, zl_ref, mu_ref, mul_ref, w0_ref, w2_ref, a0_ref, a2_ref,
         g2_ref, kk_ref, ka_ref, rk_ref, bd_ref,
         r_o, ld_o, k_o, v_o, kn_o, a_o, g_o, bon_o, cr_ref, ck_ref, cv_ref, cl_ref) = refs

    @pl.when(pl.program_id(1) == 0)
    def _():
        for cref in (cr_ref, ck_ref, cv_ref, cl_ref) + ((cd_ref,) if has_vres else ()):
            cref[...] = jnp.zeros_like(cref)

    def mixed(z_ref, carry_ref, mu):
        z = z_ref[...]
        prev = _shift_rows(z, 1, carry_ref[...])
        carry_ref[...] = z[ts - SUBLANES:, :]
        return z + (prev - z) * mu

    W = RWKV_WIDTH
    r = mixed(zr_ref, cr_ref, mu_ref[:, 0:W])
    k = mixed(zk_ref, ck_ref, mu_ref[:, W:2 * W])
    v = mixed(zv_ref, cv_ref, mu_ref[:, 2 * W:3 * W])
    lora = mixed(zl_ref, cl_ref, mul_ref[...])
    wa_in = lora[:, 0:LANES]
    gd = lora[:, LANES:2 * LANES]

    xw = w0_ref[...] + _dot(jnp.tanh(wa_in).astype(BF16), w2_ref[...])
    ld_o[...] = -jax.nn.sigmoid(xw) * math.exp(-0.5)
    a = jax.nn.sigmoid(a0_ref[...] + _dot(wa_in.astype(BF16), a2_ref[...]))
    g_o[...] = _dot(jax.nn.sigmoid(gd).astype(BF16), g2_ref[...])
    if has_vres:
        zd = mixed(zd_ref, cd_ref, vmu_ref[...])
        v = v + (vf_ref[...] - v) * jax.nn.sigmoid(v0_ref[...] + _dot(zd.astype(BF16), v2_ref[...]))
    kk = k * kk_ref[...]
    ss = _group_sum(kk * kk, bd_ref[...])
    kn_o[...] = kk / jnp.maximum(jnp.sqrt(ss), 1e-12)
    k_mod = k * (1.0 + (a - 1.0) * ka_ref[...])
    k_o[...] = k_mod
    bon_o[...] = _group_sum(r * k_mod * rk_ref[...], bd_ref[...]) * v
    r_o[...] = r
    v_o[...] = v
    a_o[...] = a


def _rwkv_prep(z, S, p, v_first):
    T = z.shape[0]
    B = T // S
    W = RWKV_WIDTH
    ts = min(512, S)
    per_b = S // ts
    has_vres = v_first is not None
    tok = lambda w, off: pl.BlockSpec((ts, w), lambda b, s: (b * per_b + s, off // w))
    const = lambda a: pl.BlockSpec(a.shape, lambda b, s: (0, 0))
    args = [z, z, z, z]
    specs = [tok(W, O_RW), tok(W, O_RW + W), tok(W, O_RW + 2 * W), tok(2 * LANES, O_LORA)]
    if has_vres:
        args += [z, v_first]
        specs += [tok(LANES, IN_COLS), tok(W, 0)]
    consts = [p["mu_rkv"], p["mu_lora"], p["w0"], p["w2"], p["a0"], p["a2"], p["g2"], p["kk"], p["ka"], p["rk"],
              p["bd"]]
    if has_vres:
        consts += [p["vmu"], p["v0"], p["v2"]]
    args += consts
    specs += [const(a) for a in consts]
    out = pl.BlockSpec((ts, W), lambda b, s: (b * per_b + s, 0))
    scratch = [pltpu.VMEM((SUBLANES, W), F32)] * 3 + [pltpu.VMEM((SUBLANES, 2 * LANES), F32)]
    if has_vres:
        scratch += [pltpu.VMEM((SUBLANES, LANES), F32)]
    return pl.pallas_call(
        functools.partial(_rwkv_prep_kernel, ts=ts, has_vres=has_vres),
        out_shape=[jax.ShapeDtypeStruct((T, W), F32)] * 8,
        grid=(B, per_b),
        in_specs=specs,
        out_specs=[out] * 8,
        scratch_shapes=scratch,
        compiler_params=_cparams(("arbitrary", "arbitrary")),
        name="rwkv7_prep",
    )(*args)


def _wkv_kernel(r_ref, ld_ref, k_ref, v_ref, kn_ref, a_ref, g_ref, bon_ref, lg_ref, lb_ref,
                o_ref, s_ref, *, n_chunks, nb):
    C = WKV_CHUNK
    N = RWKV_HEAD_DIM
    assert C == N
    GW = WKV_GROUP * N
    GR = WKV_GROUP * C
    NG = RWKV_HEADS // WKV_GROUP
    CS = C.bit_length() - 1

    @pl.when(pl.program_id(1) == 0)
    def _():
        s_ref[...] = jnp.zeros_like(s_ref)

    rowc = lax.broadcasted_iota(jnp.int32, (C, RWKV_WIDTH), 0)
    rr = lax.broadcasted_iota(jnp.int32, (GR, GW), 0)
    cc = lax.broadcasted_iota(jnp.int32, (GR, GW), 1)
    same_head = (rr >> CS) == (cc >> CS)
    m_strict = same_head & ((rr & (C - 1)) < (cc & (C - 1)))
    m_incl = same_head & ((rr & (C - 1)) <= (cc & (C - 1)))
    sel = jnp.where((lax.broadcasted_iota(jnp.int32, (C, GR), 1) & (C - 1))
                    == lax.broadcasted_iota(jnp.int32, (C, GR), 0), 1.0, 0.0).astype(BF16)

    def expand(x):
        return jnp.where(same_head, jnp.concatenate([x] * WKV_GROUP, axis=0), 0.0).astype(BF16)

    def tiled(x):
        return jnp.concatenate([x] * WKV_GROUP, axis=0).astype(BF16)

    units = [(bi, gi) for bi in range(nb) for gi in range(NG)]

    def chunk(ci, _):
        rows = pl.ds(pl.multiple_of(ci * C, C), C)
        pre = []
        for bi in range(nb):
            r, ld, k = r_ref[bi, rows, :], ld_ref[bi, rows, :], k_ref[bi, rows, :]
            kn, a = kn_ref[bi, rows, :], a_ref[bi, rows, :]
            cum = ld
            d = 1
            while d < C:
                cum = cum + jnp.where(rowc >= d, pltpu.roll(cum, d, 0), 0.0)
                d *= 2
            tot = cum[C - 1:C, :]
            e_neg = jnp.exp(-cum)
            e_end = jnp.exp(tot - cum)
            b = kn * a
            pre.append(dict(
                r_t=r * jnp.exp(cum),
                a_t=-kn * jnp.exp(cum - ld),
                b_t=b * e_neg, k_t=k * e_neg,
                b_e=b * e_end, k_e=k * e_end,
                w_end=jnp.exp(tot), v=v_ref[bi, rows, :]))

        ops = []
        for bi, gi in units:
            sl = slice(gi * GW, (gi + 1) * GW)
            p = pre[bi]
            ops.append(dict(
                sl=sl, bi=bi,
                at_x=expand(p["a_t"][:, sl]), rt_x=expand(p["r_t"][:, sl]),
                at_t=tiled(p["a_t"][:, sl]), rt_t=tiled(p["r_t"][:, sl]),
                bk_x=jnp.concatenate([expand(p["b_t"][:, sl]), expand(p["k_t"][:, sl])], axis=0),
                bh_x=expand(p["b_e"][:, sl]), kh_x=expand(p["k_e"][:, sl]),
                v_x=expand(p["v"][:, sl]), w_end=p["w_end"][:, sl],
                state=s_ref[bi, :, sl]))
        for o in ops:
            o["st_b"] = o["state"].astype(BF16)
            o["vt"] = _dot_nt(sel, o["v_x"]).astype(BF16)
        for o in ops:
            pa = _dot_nt(o["bk_x"], o["at_t"])
            pr = _dot_nt(o["bk_x"], o["rt_t"])
            o["pw"] = jnp.where(m_strict, pa[:GR], 0.0).astype(BF16)
            o["m_ak"] = jnp.where(m_strict, pa[GR:], 0.0).astype(BF16)
            o["m_rb"] = jnp.where(m_incl, pr[:GR], 0.0).astype(BF16)
            o["m_rk"] = jnp.where(m_incl, pr[GR:], 0.0).astype(BF16)
        for o in ops:
            o["u"] = _dot_nt(o["st_b"], o["at_x"]) + _dot(o["vt"], o["m_ak"])
        step = 1
        while step < C:
            last = 2 * step >= C
            for o in ops:
                u_b = o["u"].astype(BF16)
                if last:
                    o["u"] = o["u"] + _dot(u_b, o["pw"])
                else:
                    both = _dot(jnp.concatenate([o["pw"], u_b], axis=0), o["pw"])
                    o["u"] = o["u"] + both[GR:]
                    o["pw"] = both[:GR].astype(BF16)
            step *= 2
        for o in ops:
            u_b = o["u"].astype(BF16)
            out_t = (_dot_nt(o["st_b"], o["rt_x"]) + _dot(u_b, o["m_rb"])
                     + _dot(o["vt"], o["m_rk"]))
            s_ref[o["bi"], :, o["sl"]] = (o["state"] * o["w_end"] + _dot(u_b, o["bh_x"])
                                          + _dot(o["vt"], o["kh_x"]))
            mean = jnp.mean(out_t, axis=0, keepdims=True)
            cen = out_t - mean
            var = jnp.mean(cen * cen, axis=0, keepdims=True)
            o["on"] = _dot_nt(sel, expand(cen * lax.rsqrt(var + GN_EPS)))
        for bi in range(nb):
            on_wide = jnp.concatenate([o["on"] for o in ops if o["bi"] == bi], axis=1)
            o_ref[bi, rows, :] = ((on_wide * lg_ref[...] + lb_ref[...] + bon_ref[bi, rows, :])
                                  * g_ref[bi, rows, :])
        return 0

    lax.fori_loop(0, n_chunks, chunk, 0)


def _wkv_scan(r, ld, k, v, kn, a, g, bonus, S, lnx_g, lnx_b):
    T, W = r.shape
    B = T // S
    nb = max(n for n in (1, 2, 4) if B % n == 0)
    tt = min(128, S)
    seq = lambda x: x.reshape(B, S, W)
    tok = pl.BlockSpec((nb, tt, W), lambda b, s: (b, s, 0))
    rowspec = pl.BlockSpec((1, W), lambda b, s: (0, 0))
    out = pl.pallas_call(
        functools.partial(_wkv_kernel, n_chunks=tt // WKV_CHUNK, nb=nb),
        out_shape=jax.ShapeDtypeStruct((B, S, W), F32),
        grid=(B // nb, S // tt),
        in_specs=[tok] * 8 + [rowspec] * 2,
        out_specs=tok,
        scratch_shapes=[pltpu.VMEM((nb, RWKV_HEAD_DIM, W), F32)],
        compiler_params=_cparams(("arbitrary", "arbitrary")),
        name="rwkv7_chunk_scan",
    )(seq(r), seq(ld), seq(k), seq(v), seq(kn), seq(a), seq(g), seq(bonus),
      lnx_g.reshape(1, W), lnx_b.reshape(1, W))
    return out.reshape(T, W)


def _merge_kernel(x_ref, mod_ref, ga_ref, gb_ref, gc_ref, ya_ref, yb_ref, yc_ref,
                  wa_ref, wb_ref, wc_ref, wo_ref, o_ref):
    merged = (jax.nn.sigmoid(ga_ref[...]) * _dot(ya_ref[...].astype(BF16), wa_ref[...])
              + jax.nn.sigmoid(gb_ref[...]) * _dot(yb_ref[...].astype(BF16), wb_ref[...])
              + jax.nn.sigmoid(gc_ref[...]) * _dot(yc_ref[...].astype(BF16), wc_ref[...]))
    o_ref[...] = x_ref[...] + mod_ref[2:3, :] * _dot(merged.astype(BF16), wo_ref[...])


def _merge(x2, mod_l, z, ya, yb, yc, wa, wb, wc, wo, S):
    T, D = x2.shape
    tm = min(512, S)
    per_b = S // tm
    tokd = pl.BlockSpec((tm, D), lambda i: (i, 0))
    gate = lambda n: pl.BlockSpec((tm, D), lambda i: (i, n))
    tokh = pl.BlockSpec((tm, D // 2), lambda i: (i, 0))
    wh = pl.BlockSpec((D // 2, D), lambda i: (0, 0))
    return pl.pallas_call(
        _merge_kernel,
        out_shape=jax.ShapeDtypeStruct((T, D), F32),
        grid=(T // tm,),
        in_specs=[tokd, pl.BlockSpec((None, 6, D), lambda i: (i // per_b, 0, 0)),
                  gate(0), gate(1), gate(2), tokh, tokh, tokh, wh, wh, wh,
                  pl.BlockSpec((D, D), lambda i: (0, 0))],
        out_specs=tokd,
        compiler_params=_cparams(("arbitrary",)),
        name="branch_merge_out_proj",
    )(x2, mod_l, z, z, z, ya, yb, yc, wa, wb, wc, wo)


def _ffn_kernel(x_ref, mod_ref, g_ref, wug_ref, wuv_ref, cwg_ref, cwv_ref, cbg_ref, cbv_ref, wd_ref, fg_ref,
                o_ref, h_ref, acc_ref, cg_ref, cv_ref, *, tm, per_b, nf, final):
    i = pl.program_id(0)
    j = pl.program_id(1)

    @pl.when(j == 0)
    def _():
        x = x_ref[...]
        ms = jnp.mean(x * x, axis=-1, keepdims=True)
        y = x * lax.rsqrt(ms + NORM_EPS) * g_ref[...]
        h_ref[...] = (y * (1.0 + mod_ref[4:5, :]) + mod_ref[3:4, :]).astype(BF16)
        acc_ref[...] = jnp.zeros_like(acc_ref)

    @pl.when((j == 0) & (i % per_b == 0))
    def _():
        cg_ref[...] = jnp.zeros_like(cg_ref)
        cv_ref[...] = jnp.zeros_like(cv_ref)

    hb = h_ref[...]

    def branch(wu_ref, cw_ref, cb_ref, carry_ref):
        u = _dot(hb, wu_ref[...])
        carry = carry_ref[j]
        out = cb_ref[...] + cw_ref[FFN_CONV - 1:FFN_CONV, :] * u
        for d in range(1, FFN_CONV):
            out = out + cw_ref[FFN_CONV - 1 - d:FFN_CONV - d, :] * _shift_rows(u, d, carry)
        carry_ref[j] = u[tm - SUBLANES:, :]
        return out

    gate = branch(wug_ref, cwg_ref, cbg_ref, cg_ref)
    val = branch(wuv_ref, cwv_ref, cbv_ref, cv_ref)
    acc_ref[...] += _dot((_gelu_tanh(gate) * val).astype(BF16), wd_ref[...])

    @pl.when(j == nf - 1)
    def _():
        out = x_ref[...] + mod_ref[5:6, :] * acc_ref[...]
        if final:
            ms = jnp.mean(out * out, axis=-1, keepdims=True)
            out = out * lax.rsqrt(ms + NORM_EPS) * fg_ref[...]
        o_ref[...] = out


def _ffn(x2, mod_l, g, w_up, conv_w, conv_b, w_down, final_g, S, final):
    T, D = x2.shape
    F = D_FF
    tm = min(1024, S)
    tf = 512
    nf = F // tf
    per_b = S // tm
    conv_b = conv_b.reshape(1, 2 * F)
    return pl.pallas_call(
        functools.partial(_ffn_kernel, tm=tm, per_b=per_b, nf=nf, final=final),
        out_shape=jax.ShapeDtypeStruct((T, D), F32),
        grid=(T // tm, nf),
        in_specs=[pl.BlockSpec((tm, D), lambda i, j: (i, 0)),
                  pl.BlockSpec((None, 6, D), lambda i, j: (i // per_b, 0, 0)),
                  pl.BlockSpec((1, D), lambda i, j: (0, 0)),
                  pl.BlockSpec((D, tf), lambda i, j: (0, j)),
                  pl.BlockSpec((D, tf), lambda i, j: (0, nf + j)),
                  pl.BlockSpec((FFN_CONV, tf), lambda i, j: (0, j)),
                  pl.BlockSpec((FFN_CONV, tf), lambda i, j: (0, nf + j)),
                  pl.BlockSpec((1, tf), lambda i, j: (0, j)),
                  pl.BlockSpec((1, tf), lambda i, j: (0, nf + j)),
                  pl.BlockSpec((tf, D), lambda i, j: (j, 0)),
                  pl.BlockSpec((1, D), lambda i, j: (0, 0))],
        out_specs=pl.BlockSpec((tm, D), lambda i, j: (i, 0)),
        scratch_shapes=[pltpu.VMEM((tm, D), BF16), pltpu.VMEM((tm, D), F32),
                        pltpu.VMEM((nf, SUBLANES, tf), F32), pltpu.VMEM((nf, SUBLANES, tf), F32)],
        compiler_params=_cparams(("arbitrary", "arbitrary")),
        name="conv_gated_mlp",
    )(x2, mod_l, g.reshape(1, D), w_up, w_up, conv_w, conv_w, conv_b, conv_b, w_down, final_g.reshape(1, D))


def _block_diag(w):
    n, c, d = w.shape
    return jnp.einsum("ncd,nm->ncmd", w, jnp.eye(n, dtype=w.dtype)).reshape(n * c, n * d)


def _pad_rows(w, rows, offset):
    return jnp.zeros((rows, w.shape[1]), w.dtype).at[offset:offset + w.shape[0]].set(w)


def kernel(x, c, positions, w_mod, b_mod, norm1_g, norm2_g, final_g, w_in, w_merge_a, w_merge_b, w_merge_c, w_out, lru_conv_w, lru_conv_b, lru_wa, lru_ba, lru_wx, lru_bx, lru_lambda, diff_lq1, diff_lk1, diff_lq2, diff_lk2, diff_subln_g, rwkv_mu, rwkv_w0, rwkv_w2, rwkv_a0, rwkv_a2, rwkv_g2, rwkv_kk, rwkv_ka, rwkv_rk, rwkv_lnx_g, rwkv_lnx_b, rwkv_v0, rwkv_v1, rwkv_v2, rwkv_vmu, ffn_up, ffn_conv_w, ffn_conv_b, ffn_down):
    B, S, D = x.shape
    L = w_in.shape[0]
    T = B * S
    W = RWKV_WIDTH
    x2 = x.reshape(T, D)

    mod = _modulation(c, w_mod, b_mod).reshape(L, B, 6, D)
    cos_t, sa_t, sb_t = _rope_tables(positions)
    head_ones = _block_diag(jnp.ones((RWKV_HEADS, RWKV_HEAD_DIM, RWKV_HEAD_DIM), BF16))
    row = lambda a: a.reshape(1, -1)

    v_first = None
    for l in range(L):
        w_cat = w_in[l] if l == 0 else jnp.concatenate([w_in[l], rwkv_v1[l - 1]], axis=1)
        w_cat = jnp.pad(w_cat, ((0, 0), (0, IN_COLS_PAD - w_cat.shape[1]))).astype(BF16)
        z = _in_projection(x2, mod[l], norm1_g[l], w_cat, S)

        ya = _rglru(z, S, lru_conv_w[l], lru_conv_b[l], _block_diag(lru_wa[l]).astype(BF16), lru_ba[l],
                    _block_diag(lru_wx[l]).astype(BF16), lru_bx[l], lru_lambda[l])

        qh, kh, vt = _qkv_prep(z, cos_t, sa_t, sb_t, S)
        lambda_init = 0.8 - 0.6 * math.exp(-0.3 * l)
        yb = _diff_attention(qh, kh, vt, S, diff_lq1[l], diff_lk1[l], diff_lq2[l], diff_lk2[l],
                             diff_subln_g[l], lambda_init)

        mu = rwkv_mu[l]
        prm = {
            "mu_rkv": row(mu[:3 * W]), "mu_lora": row(mu[3 * W:]),
            "w0": row(rwkv_w0[l]), "w2": _pad_rows(rwkv_w2[l], LANES, 0).astype(BF16),
            "a0": row(rwkv_a0[l]), "a2": _pad_rows(rwkv_a2[l], LANES, DECAY_LORA).astype(BF16),
            "g2": rwkv_g2[l].astype(BF16), "kk": row(rwkv_kk[l]), "ka": row(rwkv_ka[l]), "rk": row(rwkv_rk[l]),
            "bd": head_ones,
        }
        if l > 0:
            prm["vmu"] = jnp.pad(row(rwkv_vmu[l - 1]), ((0, 0), (0, LANES - MV_LORA)))
            prm["v0"] = row(rwkv_v0[l - 1])
            prm["v2"] = _pad_rows(rwkv_v2[l - 1], LANES, 0).astype(BF16)
        r, ld, k, v, kn, a, g, bonus = _rwkv_prep(z, S, prm, v_first)
        if l == 0:
            v_first = v
        yc = _wkv_scan(r, ld, k, v, kn, a, g, bonus, S, rwkv_lnx_g[l], rwkv_lnx_b[l])

        x2 = _merge(x2, mod[l], z, ya, yb, yc, w_merge_a[l].astype(BF16), w_merge_b[l].astype(BF16),
                    w_merge_c[l].astype(BF16), w_out[l].astype(BF16), S)
        x2 = _ffn(x2, mod[l], norm2_g[l], ffn_up[l].astype(BF16), ffn_conv_w[l], ffn_conv_b[l],
                  ffn_down[l].astype(BF16), final_g, S, final=(l == L - 1))
    return x2.reshape(B, S, D)
```
